```python
import math
import jax, jax.numpy as jnp
from jax import lax
import numpy as np

D_MODEL = 1024
BATCH = 32
SEQ = 2048
DEPTH = 4

N_EVEN = (DEPTH + 1) // 2
N_ODD = DEPTH // 2
NORM_EPS = 1e-6
CHUNK = 64
S5_WIDTH = D_MODEL // 2
S5_GROUP_SIZE = 16
S5_GROUPS = S5_WIDTH // S5_GROUP_SIZE
S5_STATE = 64
DT_MIN = 1e-3
DT_MAX = 1e-1
GLA_HEADS = 4
GLA_VW = D_MODEL // 2
GLA_KW = GLA_VW // 2
GLA_DV_HEAD = GLA_VW // GLA_HEADS
GLA_DK_HEAD = GLA_KW // GLA_HEADS
GLA_RANK = 16
GLA_GATE_TEMP = 16.0
EVEN_MIX_WIDTH = S5_WIDTH + GLA_VW
EVEN_PROJ = S5_WIDTH + 2 * GLA_KW + 2 * GLA_VW + GLA_RANK
MLSTM_INNER = 2 * D_MODEL
MLSTM_HEADS = 4
MLSTM_HEAD_DIM = MLSTM_INNER // MLSTM_HEADS
MLSTM_CONV = 4
MLSTM_QKV_BLOCK = 4
MLSTM_NBLOCKS = MLSTM_INNER // MLSTM_QKV_BLOCK
FFN_DIM = 2816
FFN_CONV = 3

kernel_name = "hybrid_s5_gla_mlstm_trunk"


def rms_norm(x, g):
    xf = x.astype(jnp.float32)
    y = xf * lax.rsqrt(jnp.mean(xf * xf, axis=-1, keepdims=True) + NORM_EPS)
    return (y * g.astype(jnp.float32)).astype(x.dtype)


def head_rms_norm(x, g, n_heads):
    b, l, w = x.shape
    xh = x.astype(jnp.float32).reshape(b, l, n_heads, w // n_heads)
    y = xh * lax.rsqrt(jnp.mean(xh * xh, axis=-1, keepdims=True) + NORM_EPS)
    return y.reshape(b, l, w) * g.astype(jnp.float32)


def causal_dwconv(x, w, b):
    width = w.shape[0]
    seq = x.shape[1]
    xp = jnp.pad(x, ((0, 0), (width - 1, 0), (0, 0)))
    out = b
    for k in range(width):
        out = out + xp[:, k:k + seq] * w[k]
    return out


def to_chunks(t):
    b, l = t.shape[:2]
    return jnp.moveaxis(t.reshape((b, l // CHUNK, CHUNK) + t.shape[2:]), 1, 0)


def from_chunks(t):
    n, b, c = t.shape[:3]
    return jnp.moveaxis(t, 0, 1).reshape((b, n * c) + t.shape[3:])


def _complex_affine_combine(e1, e2):
    a1r, a1i, b1r, b1i = e1
    a2r, a2i, b2r, b2i = e2
    return (a2r * a1r - a2i * a1i,
            a2r * a1i + a2i * a1r,
            a2r * b1r - a2i * b1i + b2r,
            a2r * b1i + a2i * b1r + b2i)


def s5_mixer(u, lam_re, lam_im, b_re, b_im, c_re, c_im, d_skip, log_dt, w_glu, b_glu):
    f32 = jnp.float32
    bsz, seq, _ = u.shape
    ug = u.astype(f32).reshape(bsz, seq, S5_GROUPS, S5_GROUP_SIZE)
    dt = jnp.exp(log_dt.astype(f32))[:, None]
    lr = lam_re.astype(f32)
    li = lam_im.astype(f32)
    mag = jnp.exp(lr * dt)
    lb_re = mag * jnp.cos(li * dt)
    lb_im = mag * jnp.sin(li * dt)
    inv = 1.0 / (lr * lr + li * li)
    zr = ((lb_re - 1.0) * lr + lb_im * li) * inv
    zi = (lb_im * lr - (lb_re - 1.0) * li) * inv
    br = b_re.astype(f32)
    bi = b_im.astype(f32)
    bb_re = zr[..., None] * br - zi[..., None] * bi
    bb_im = zr[..., None] * bi + zi[..., None] * br
    bu_re = jnp.einsum('blgi,gpi->blgp', ug, bb_re)
    bu_im = jnp.einsum('blgi,gpi->blgp', ug, bb_im)
    shape = (1, seq, S5_GROUPS, S5_STATE)
    a_re = jnp.broadcast_to(lb_re, shape)
    a_im = jnp.broadcast_to(lb_im, shape)
    _, _, x_re, x_im = lax.associative_scan(
        _complex_affine_combine, (a_re, a_im, bu_re, bu_im), axis=1)
    y = (jnp.einsum('blgp,gip->blgi', x_re, c_re.astype(f32))
         - jnp.einsum('blgp,gip->blgi', x_im, c_im.astype(f32))
         + d_skip.astype(f32) * ug)
    y = jax.nn.gelu(y.reshape(bsz, seq, S5_WIDTH), approximate=True)
    return y * jax.nn.sigmoid(y @ w_glu.astype(f32) + b_glu.astype(f32))


def gla_mixer(q, k, v, r, a_lr, w_alpha_up, b_alpha, norm_g):
    f32 = jnp.float32
    bsz, seq, _ = q.shape
    q = q.astype(f32).reshape(bsz, seq, GLA_HEADS, GLA_DK_HEAD) * (GLA_DK_HEAD ** -0.5)
    k = k.astype(f32).reshape(bsz, seq, GLA_HEADS, GLA_DK_HEAD)
    v = v.astype(f32).reshape(bsz, seq, GLA_HEADS, GLA_DV_HEAD)
    log_a = jax.nn.log_sigmoid(a_lr.astype(f32) @ w_alpha_up.astype(f32)
                               + b_alpha.astype(f32)) / GLA_GATE_TEMP
    log_a = log_a.reshape(bsz, seq, GLA_HEADS, GLA_DK_HEAD)
    qc, kc, vc = to_chunks(q), to_chunks(k), to_chunks(v)
    gc = jnp.cumsum(to_chunks(log_a), axis=2)
    mask = jnp.tril(jnp.ones((CHUNK, CHUNK), dtype=bool))

    def step(state, inp):
        qi, ki, vi, gi = inp
        g_last = gi[:, -1]
        q_dec = qi * jnp.exp(gi)
        k_dec = ki * jnp.exp(-gi)
        k_upd = ki * jnp.exp(g_last[:, None] - gi)
        scores = jnp.where(mask, jnp.einsum('bihd,bjhd->bhij', q_dec, k_dec), 0.0)
        out = (jnp.einsum('bhij,bjhv->bihv', scores, vi)
               + jnp.einsum('bihd,bhdv->bihv', q_dec, state))
        state = jnp.exp(g_last)[..., None] * state + jnp.einsum('bjhd,bjhv->bhdv', k_upd, vi)
        return state, out

    s0 = jnp.zeros((bsz, GLA_HEADS, GLA_DK_HEAD, GLA_DV_HEAD), f32)
    _, oc = lax.scan(step, s0, (qc, kc, vc, gc))
    o = from_chunks(oc).reshape(bsz, seq, GLA_VW)
    return head_rms_norm(o, norm_g, GLA_HEADS) * jax.nn.silu(r.astype(f32))


def headwise(x, w):
    b, l, _ = x.shape
    xb = x.reshape(b, l, MLSTM_NBLOCKS, MLSTM_QKV_BLOCK)
    return jnp.einsum('blnc,ncd->blnd', xb, w.astype(jnp.float32)).reshape(b, l, MLSTM_INNER)


def mlstm_mixer(x_m, o_pre, conv_w, conv_b, w_q, w_k, w_v, w_gate, b_gate, norm_g, skip):
    f32 = jnp.float32
    bsz, seq, _ = x_m.shape
    H, DH = MLSTM_HEADS, MLSTM_HEAD_DIM
    xm = x_m.astype(f32)
    xc = jax.nn.silu(causal_dwconv(xm, conv_w.astype(f32), conv_b.astype(f32)))
    q = headwise(xc, w_q)
    k = headwise(xc, w_k)
    v = headwise(xm, w_v)
    wg = w_gate.astype(f32)
    gates = (jnp.einsum('bld,dg->blg', q, wg[0]) + jnp.einsum('bld,dg->blg', k, wg[1])
             + jnp.einsum('bld,dg->blg', v, wg[2]) + b_gate.astype(f32))
    log_i = gates[..., :H]
    log_f = jax.nn.log_sigmoid(gates[..., H:])
    qh = q.reshape(bsz, seq, H, DH)
    kh = k.reshape(bsz, seq, H, DH) * (DH ** -0.5)
    vh = v.reshape(bsz, seq, H, DH)
    mask = jnp.tril(jnp.ones((CHUNK, CHUNK), dtype=bool))

    def step(carry, inp):
        c_mat, n_vec, m = carry
        qi, ki, vi, ii, fi = inp
        f_cum = jnp.cumsum(fi, axis=1).transpose(0, 2, 1)
        i_t = ii.transpose(0, 2, 1)
        d_log = jnp.where(mask, f_cum[..., :, None] - f_cum[..., None, :] + i_t[..., None, :],
                          -jnp.inf)
        inter_log = f_cum + m[..., None]
        m_loc = jnp.maximum(inter_log, jnp.max(d_log, axis=-1))
        s = jnp.einsum('bihd,bjhd->bhij', qi, ki) * jnp.exp(d_log - m_loc[..., None])
        w_inter = jnp.exp(inter_log - m_loc)
        num = (jnp.einsum('bhij,bjhe->bihe', s, vi)
               + w_inter.transpose(0, 2, 1)[..., None] * jnp.einsum('bihd,bhed->bihe', qi, c_mat))
        den = jnp.sum(s, axis=-1) + w_inter * jnp.einsum('bihd,bhd->bhi', qi, n_vec)
        denom = jnp.maximum(jnp.abs(den), jnp.exp(-m_loc)).transpose(0, 2, 1)[..., None]
        h = num / denom
        f_last = f_cum[..., -1]
        w_log = f_last[..., None] - f_cum + i_t
        m_new = jnp.maximum(f_last + m, jnp.max(w_log, axis=-1))
        w_upd = jnp.exp(w_log - m_new[..., None]).transpose(0, 2, 1)[..., None]
        decay = jnp.exp(f_last + m - m_new)
        c_mat = decay[..., None, None] * c_mat + jnp.einsum('bjhe,bjhd->bhed', vi * w_upd, ki)
        n_vec = decay[..., None] * n_vec + jnp.sum(ki * w_upd, axis=1)
        return (c_mat, n_vec, m_new), h

    init = (jnp.zeros((bsz, H, DH, DH), f32), jnp.zeros((bsz, H, DH), f32),
            jnp.zeros((bsz, H), f32))
    _, hc = lax.scan(step, init, (to_chunks(qh), to_chunks(kh), to_chunks(vh),
                                  to_chunks(log_i), to_chunks(log_f)))
    h = from_chunks(hc).reshape(bsz, seq, MLSTM_INNER)
    h = head_rms_norm(h, norm_g, H) + skip.astype(f32) * xc
    return jax.nn.sigmoid(o_pre.astype(f32)) * h


def conv_ffn(h, w_up, conv_w, conv_b, w_down):
    g, u = jnp.split(h @ w_up, 2, axis=-1)
    g = causal_dwconv(g, conv_w, conv_b)
    return ((jax.nn.gelu(g, approximate=True) * u) @ w_down).astype(h.dtype)


def setup_inputs(seed: int = 0) -> dict:
    key = jax.random.key(seed)
    ks = jax.random.split(key, 40)
    f32 = jnp.float32

    def nrm(i, shape, scale):
        return scale * jax.random.normal(ks[i], shape, f32)

    def gain(i, shape):
        return 1.0 + 0.01 * jax.random.normal(ks[i], shape, f32)

    NE, NO, G, P, GS = N_EVEN, N_ODD, S5_GROUPS, S5_STATE, S5_GROUP_SIZE
    lam_im = math.pi * jnp.arange(P, dtype=f32) + nrm(7, (NE, G, P), 0.01)
    b_gate = jnp.concatenate(
        [nrm(26, (NO, MLSTM_HEADS), 0.1),
         jnp.linspace(3.0, 6.0, MLSTM_HEADS, dtype=f32) + nrm(27, (NO, MLSTM_HEADS), 0.01)],
        axis=-1)
    return {
        "x": nrm(0, (BATCH, SEQ, D_MODEL), 1.0),
        "ln_mix_pre": gain(1, (DEPTH, D_MODEL)),
        "ln_mix_post": gain(2, (DEPTH, D_MODEL)),
        "ln_ffn_pre": gain(3, (DEPTH, D_MODEL)),
        "ln_ffn_post": gain(4, (DEPTH, D_MODEL)),
        "ev_w_in": nrm(5, (NE, D_MODEL, EVEN_PROJ), D_MODEL ** -0.5),
        "s5_lambda_re": -0.5 + nrm(6, (NE, G, P), 0.01),
        "s5_lambda_im": lam_im,
        "s5_b_re": nrm(8, (NE, G, P, GS), (2 * GS) ** -0.5),
        "s5_b_im": nrm(9, (NE, G, P, GS), (2 * GS) ** -0.5),
        "s5_c_re": nrm(10, (NE, G, GS, P), P ** -0.5),
        "s5_c_im": nrm(11, (NE, G, GS, P), P ** -0.5),
        "s5_d": nrm(12, (NE, G, GS), 1.0),
        "s5_log_dt": jax.random.uniform(ks[13], (NE, G), f32, minval=math.log(DT_MIN),
                                        maxval=math.log(DT_MAX)),
        "s5_w_glu": nrm(14, (NE, S5_WIDTH, S5_WIDTH), S5_WIDTH ** -0.5),
        "s5_b_glu": nrm(15, (NE, S5_WIDTH), 0.01),
        "gla_w_alpha_up": nrm(16, (NE, GLA_RANK, GLA_KW), GLA_RANK ** -0.5),
        "gla_b_alpha": nrm(17, (NE, GLA_KW), 0.1),
        "gla_norm": gain(18, (NE, GLA_VW)),
        "ev_w_out": nrm(19, (NE, EVEN_MIX_WIDTH, D_MODEL), EVEN_MIX_WIDTH ** -0.5),
        "od_w_in": nrm(20, (NO, D_MODEL, 2 * MLSTM_INNER), D_MODEL ** -0.5),
        "ml_conv_w": nrm(21, (NO, MLSTM_CONV, MLSTM_INNER), MLSTM_CONV ** -0.5),
        "ml_conv_b": nrm(22, (NO, MLSTM_INNER), 0.01),
        "ml_w_q": nrm(23, (NO, MLSTM_NBLOCKS, MLSTM_QKV_BLOCK, MLSTM_QKV_BLOCK), MLSTM_QKV_BLOCK ** -0.5),
        "ml_w_k": nrm(24, (NO, MLSTM_NBLOCKS, MLSTM_QKV_BLOCK, MLSTM_QKV_BLOCK), MLSTM_QKV_BLOCK ** -0.5),
        "ml_w_v": nrm(25, (NO, MLSTM_NBLOCKS, MLSTM_QKV_BLOCK, MLSTM_QKV_BLOCK), MLSTM_QKV_BLOCK ** -0.5),
        "ml_w_gate": nrm(28, (NO, 3, MLSTM_INNER, 2 * MLSTM_HEADS), (3 * MLSTM_INNER) ** -0.5),
        "ml_b_gate": b_gate,
        "ml_norm": gain(29, (NO, MLSTM_INNER)),
        "ml_skip": gain(30, (NO, MLSTM_INNER)),
        "od_w_out": nrm(31, (NO, MLSTM_INNER, D_MODEL), MLSTM_INNER ** -0.5),
        "ffn_w_up": nrm(32, (DEPTH, D_MODEL, 2 * FFN_DIM), D_MODEL ** -0.5),
        "ffn_conv_w": nrm(33, (DEPTH, FFN_CONV, FFN_DIM), FFN_CONV ** -0.5),
        "ffn_conv_b": nrm(34, (DEPTH, FFN_DIM), 0.01),
        "ffn_w_down": nrm(35, (DEPTH, FFN_DIM, D_MODEL), FFN_DIM ** -0.5),
    }


def reference(x, ln_mix_pre, ln_mix_post, ln_ffn_pre, ln_ffn_post,
              ev_w_in, s5_lambda_re, s5_lambda_im, s5_b_re, s5_b_im, s5_c_re, s5_c_im,
              s5_d, s5_log_dt, s5_w_glu, s5_b_glu, gla_w_alpha_up, gla_b_alpha, gla_norm,
              ev_w_out, od_w_in, ml_conv_w, ml_conv_b, ml_w_q, ml_w_k, ml_w_v, ml_w_gate,
              ml_b_gate, ml_norm, ml_skip, od_w_out, ffn_w_up, ffn_conv_w, ffn_conv_b,
              ffn_w_down):
    splits = [S5_WIDTH, S5_WIDTH + GLA_KW, S5_WIDTH + 2 * GLA_KW,
              S5_WIDTH + 2 * GLA_KW + GLA_VW, S5_WIDTH + 2 * GLA_KW + 2 * GLA_VW]
    for layer in range(DEPTH):
        h = rms_norm(x, ln_mix_pre[layer])
        if layer % 2 == 0:
            e = layer // 2
            p = h @ ev_w_in[e]
            u, q, k, v, r, a_lr = jnp.split(p, splits, axis=-1)
            a_out = s5_mixer(u, s5_lambda_re[e], s5_lambda_im[e], s5_b_re[e], s5_b_im[e],
                             s5_c_re[e], s5_c_im[e], s5_d[e], s5_log_dt[e], s5_w_glu[e],
                             s5_b_glu[e])
            b_out = gla_mixer(q, k, v, r, a_lr, gla_w_alpha_up[e], gla_b_alpha[e], gla_norm[e])
            mix = (jnp.concatenate([a_out, b_out], axis=-1) @ ev_w_out[e]).astype(h.dtype)
        else:
            o = layer // 2
            x_m, o_pre = jnp.split(h @ od_w_in[o], 2, axis=-1)
            c_out = mlstm_mixer(x_m, o_pre, ml_conv_w[o], ml_conv_b[o], ml_w_q[o], ml_w_k[o],
                                ml_w_v[o], ml_w_gate[o], ml_b_gate[o], ml_norm[o], ml_skip[o])
            mix = (c_out @ od_w_out[o]).astype(h.dtype)
        x = x + rms_norm(mix, ln_mix_post[layer])
        h = rms_norm(x, ln_ffn_pre[layer])
        f = conv_ffn(h, ffn_w_up[layer], ffn_conv_w[layer], ffn_conv_b[layer], ffn_w_down[layer])
        x = x + rms_norm(f, ln_ffn_post[layer])
    return x
```

```python
import functools
import math

import jax
import jax.numpy as jnp
from jax import lax
from jax.experimental import pallas as pl
from jax.experimental.pallas import tpu as pltpu

F32 = jnp.float32
BF16 = jnp.bfloat16

NORM_EPS = 1e-6
DT_UNUSED = None

S5_GROUP_SIZE = 16
S5_STATE = 64
GLA_HEADS = 4
GLA_CHUNK = 64
GLA_GATE_TEMP = 16.0
MLSTM_HEADS = 4
MLSTM_QKV_BLOCK = 4

LANES = 128
SUBLANES = 8
MXU_DIM = 256
VMEM_LIMIT_BYTES = 58 * 1024 * 1024

EVEN_TILE = 512
S5_STEPS = 32
ODD_TILE = 256
FFN_TILE = 512
FFN_COLS = 256
S5_LANE_GROUPS = LANES // S5_GROUP_SIZE


def _dot(a, b):
    return jnp.dot(a, b, preferred_element_type=F32)


def _dot_nt(a, b):
    return lax.dot_general(a, b, (((1,), (1,)), ((), ())), preferred_element_type=F32)


def _dot_tn(a, b):
    return lax.dot_general(a, b, (((0,), (0,)), ((), ())), preferred_element_type=F32)


def _rms(x, g):
    ms = jnp.mean(x * x, axis=-1, keepdims=True)
    return x * lax.rsqrt(ms + NORM_EPS) * g


def _log_sigmoid(x):
    return jnp.minimum(x, 0.0) - jnp.log1p(jnp.exp(-jnp.abs(x)))


def _split_dot(mask_bf16, x):
    hi = x.astype(BF16)
    lo = (x - hi.astype(F32)).astype(BF16)
    return _dot(mask_bf16, hi) + _dot(mask_bf16, lo)


def _full_spec(arr):
    nd = arr.ndim
    return pl.BlockSpec(arr.shape, lambda b, l, _nd=nd: (0,) * _nd,
                        pipeline_mode=pl.Buffered(1))


def _tile_spec(tile, d):
    return pl.BlockSpec((None, tile, d), lambda b, l: (b, l, 0))


def _compiler_params():
    return pltpu.CompilerParams(dimension_semantics=("arbitrary", "arbitrary"),
                                vmem_limit_bytes=VMEM_LIMIT_BYTES)


def _ffn_kernel(x_ref, lnpre_ref, lnpost_ref, wg_ref, wu_ref, cw_ref, cb_ref, wd_ref,
                o_ref, h_ref, acc_ref, tail_ref):
    tile = x_ref.shape[0]
    n_chunks = wg_ref.shape[0]

    @pl.when(pl.program_id(1) == 0)
    def _():
        tail_ref[...] = jnp.zeros_like(tail_ref)

    x = x_ref[...]
    h_ref[...] = _rms(x, lnpre_ref[...]).astype(BF16)
    acc_ref[...] = jnp.zeros_like(acc_ref)

    def conv3(g, w):
        return (w[0:1] * pltpu.roll(g, 2, axis=0) + w[1:2] * pltpu.roll(g, 1, axis=0)
                + w[2:3] * g)

    def body(c, carry):
        h = h_ref[...]
        g = _dot(h, wg_ref[c])
        u = _dot(h, wu_ref[c])
        w = cw_ref[c]
        b = cb_ref[c]
        conv = conv3(g, w) + b
        head = jnp.concatenate([tail_ref[c], g[0:SUBLANES]], axis=0)
        conv_head = (conv3(head, w) + b)[SUBLANES:2 * SUBLANES]
        tail_ref[c] = g[tile - SUBLANES:tile]
        conv = jnp.concatenate([conv_head, conv[SUBLANES:]], axis=0)
        act = (jax.nn.gelu(conv, approximate=True) * u).astype(BF16)
        acc_ref[...] += _dot(act, wd_ref[c])
        return carry

    lax.fori_loop(0, n_chunks, body, 0)
    o_ref[...] = x + _rms(acc_ref[...], lnpost_ref[...])


def _ffn_layer(x, ln_pre, ln_post, w_up, conv_w, conv_b, w_down):
    bsz, seq, d = x.shape
    f = w_down.shape[0]
    tile = min(FFN_TILE, seq)
    nc = f // FFN_COLS
    wg = w_up[:, :f].reshape(d, nc, FFN_COLS).transpose(1, 0, 2).astype(BF16)
    wu = w_up[:, f:].reshape(d, nc, FFN_COLS).transpose(1, 0, 2).astype(BF16)
    wd = w_down.reshape(nc, FFN_COLS, d).astype(BF16)
    taps = conv_w.shape[0]
    cw = jnp.zeros((SUBLANES, f), F32).at[:taps].set(conv_w)
    cw = cw.reshape(SUBLANES, nc, FFN_COLS).transpose(1, 0, 2)
    cb = conv_b.reshape(nc, 1, FFN_COLS)
    args = (ln_pre.reshape(1, d), ln_post.reshape(1, d), wg, wu, cw, cb, wd)
    return pl.pallas_call(
        _ffn_kernel,
        grid=(bsz, seq // tile),
        in_specs=[_tile_spec(tile, d)] + [_full_spec(a) for a in args],
        out_specs=_tile_spec(tile, d),
        out_shape=jax.ShapeDtypeStruct(x.shape, x.dtype),
        scratch_shapes=[pltpu.VMEM((tile, d), BF16), pltpu.VMEM((tile, d), F32),
                        pltpu.VMEM((nc, SUBLANES, FFN_COLS), F32)],
        compiler_params=_compiler_params(),
        name="conv_ffn",
    )(x, *args)


def _even_kernel(x_ref, lnpre_ref, lnpost_ref, wmain_ref, wa_ref, wup_ref, balpha_ref, gnorm_ref,
                 lam_ref, lamc_ref, bmat_ref, cmat_ref, dskip_ref, wglu_ref, bglu_ref, wout_ref,
                 o_ref,
                 h_ref, unat_ref, uperm_ref, bu_ref, xs_ref, yperm_ref, e_ref, cin_ref, anat_ref,
                 s5carry_ref, ogla_ref, glastate_ref):
    tile = x_ref.shape[0]
    n_blk = unat_ref.shape[0]
    sw = n_blk * LANES
    half = bu_ref.shape[1] // 2
    steps = S5_STEPS
    rows = tile // steps
    kw = wup_ref.shape[1]
    vw = ogla_ref.shape[1]
    dk = kw // GLA_HEADS
    dv = vw // GLA_HEADS
    n_chunks = tile // GLA_CHUNK

    @pl.when(pl.program_id(1) == 0)
    def _():
        s5carry_ref[...] = jnp.zeros_like(s5carry_ref)
        glastate_ref[...] = jnp.zeros_like(glastate_ref)

    x = x_ref[...]
    h_ref[...] = _rms(x, lnpre_ref[...]).astype(BF16)
    h = h_ref[...]

    for blk in range(n_blk):
        unat_ref[blk] = _dot(h, wmain_ref[:, blk * LANES:(blk + 1) * LANES])
    for blk in range(n_blk):
        for s in range(steps):
            uperm_ref[s * rows:(s + 1) * rows, blk * LANES:(blk + 1) * LANES] = (
                unat_ref[blk, pl.ds(s, rows, stride=steps), :])

    for blk in range(n_blk):
        lanes = slice(blk * LANES, (blk + 1) * LANES)
        bu_ref[...] = _dot(uperm_ref[:, lanes].astype(BF16), bmat_ref[blk])
        a_re = lam_ref[blk, 0:1, :]
        a_im = lam_ref[blk, 1:2, :]

        def advance(s, state, store):
            s_re, s_im = state
            off = pl.multiple_of(s * rows, rows)
            b = bu_ref[pl.ds(off, rows), :]
            n_re = a_re * s_re - a_im * s_im + b[:, :half]
            n_im = a_re * s_im + a_im * s_re + b[:, half:]
            if store:
                xs_ref[pl.ds(off, rows), :] = jnp.concatenate([n_re, n_im], axis=1)
            return n_re, n_im

        zero = jnp.zeros((rows, half), F32)
        e_re, e_im = lax.fori_loop(0, steps, functools.partial(advance, store=False), (zero, zero))
        e_ref[...] = jnp.concatenate([e_re, e_im], axis=1)
        ac_re = lamc_ref[blk, 0:1, :]
        ac_im = lamc_ref[blk, 1:2, :]
        c_re = s5carry_ref[blk, 0:1, :]
        c_im = s5carry_ref[blk, 1:2, :]
        for i in range(rows):
            cin_ref[i:i + 1, :] = jnp.concatenate([c_re, c_im], axis=1)
            e_row = e_ref[i:i + 1, :]
            c_re, c_im = (ac_re * c_re - ac_im * c_im + e_row[:, :half],
                          ac_re * c_im + ac_im * c_re + e_row[:, half:])
        s5carry_ref[blk, 0:1, :] = c_re
        s5carry_ref[blk, 1:2, :] = c_im
        cin = cin_ref[...]
        lax.fori_loop(0, steps, functools.partial(advance, store=True), (cin[:, :half], cin[:, half:]))
        yperm_ref[:, lanes] = _dot(xs_ref[...].astype(BF16), cmat_ref[blk])

    y = yperm_ref[...] + dskip_ref[...] * uperm_ref[...]
    y = jax.nn.gelu(y, approximate=True)
    a_perm = y * jax.nn.sigmoid(_dot(y.astype(BF16), wglu_ref[...]) + bglu_ref[...])
    uperm_ref[...] = a_perm
    for blk in range(n_blk):
        for s in range(steps):
            unat_ref[blk, pl.ds(s, rows, stride=steps), :] = (
                uperm_ref[s * rows:(s + 1) * rows, blk * LANES:(blk + 1) * LANES])
    for blk in range(n_blk):
        anat_ref[:, blk * LANES:(blk + 1) * LANES] = unat_ref[blk].astype(BF16)

    o0 = sw
    q = _dot(h, wmain_ref[:, o0:o0 + kw]) * (dk ** -0.5)
    k = _dot(h, wmain_ref[:, o0 + kw:o0 + 2 * kw])
    v = _dot(h, wmain_ref[:, o0 + 2 * kw:o0 + 2 * kw + vw]).astype(BF16)
    a_lr = _dot(h, wa_ref[...])
    z = _dot(a_lr.astype(BF16), wup_ref[...]) + balpha_ref[...]
    log_a = _log_sigmoid(z) * (1.0 / GLA_GATE_TEMP)

    row = lax.broadcasted_iota(jnp.int32, (tile, tile), 0)
    col = lax.broadcasted_iota(jnp.int32, (tile, tile), 1)
    chunk_start = jnp.bitwise_and(row, -GLA_CHUNK)
    in_chunk = jnp.where(col >= chunk_start, jnp.where(col < chunk_start + GLA_CHUNK, 1.0, 0.0), 0.0)
    causal = jnp.where(col <= row, in_chunk, 0.0)
    g = _split_dot(causal.astype(BF16), log_a)
    g_last = _split_dot(in_chunk.astype(BF16), log_a)
    q_dec = (q * jnp.exp(g)).astype(BF16)
    k_dec = (k * jnp.exp(-g)).astype(BF16)
    k_upd = (k * jnp.exp(g_last - g)).astype(BF16)
    decay = jnp.exp(g_last)
    keep = causal > 0.5

    for hd in range(GLA_HEADS):
        ks = slice(hd * dk, (hd + 1) * dk)
        vs = slice(hd * dv, (hd + 1) * dv)
        qh, kh, kuh, vh = q_dec[:, ks], k_dec[:, ks], k_upd[:, ks], v[:, vs]
        scores = jnp.where(keep, _dot_nt(qh, kh), 0.0).astype(BF16)
        o_intra = _dot(scores, vh)
        st = glastate_ref[hd]
        for c in range(n_chunks):
            rs = slice(c * GLA_CHUNK, (c + 1) * GLA_CHUNK)
            o_inter = _dot_nt(qh[rs], st.astype(BF16))
            ogla_ref[rs, vs] = o_intra[rs] + o_inter
            d_st = _dot_tn(vh[rs], kuh[rs])
            st = st * decay[c * GLA_CHUNK:c * GLA_CHUNK + 1, ks] + d_st
        glastate_ref[hd] = st

    r = _dot(h, wmain_ref[:, o0 + 2 * kw + vw:o0 + 2 * kw + 2 * vw])
    gate = jax.nn.silu(r)
    for hd in range(GLA_HEADS):
        vs = slice(hd * dv, (hd + 1) * dv)
        oh = ogla_ref[:, vs]
        ms = jnp.mean(oh * oh, axis=-1, keepdims=True)
        ogla_ref[:, vs] = oh * lax.rsqrt(ms + NORM_EPS) * gnorm_ref[:, vs] * gate[:, vs]

    mix = (_dot(anat_ref[...], wout_ref[0:sw, :])
           + _dot(ogla_ref[...].astype(BF16), wout_ref[sw:sw + vw, :]))
    o_ref[...] = x + _rms(mix, lnpost_ref[...])


def _s5_discretize(lam_re, lam_im, b_re, b_im, log_dt, power):
    dt = jnp.exp(log_dt)[:, None]
    mag = jnp.exp(lam_re * dt)
    lb_re = mag * jnp.cos(lam_im * dt)
    lb_im = mag * jnp.sin(lam_im * dt)
    inv = 1.0 / (lam_re * lam_re + lam_im * lam_im)
    zr = ((lb_re - 1.0) * lam_re + lb_im * lam_im) * inv
    zi = (lb_im * lam_re - (lb_re - 1.0) * lam_im) * inv
    bb_re = zr[..., None] * b_re - zi[..., None] * b_im
    bb_im = zr[..., None] * b_im + zi[..., None] * b_re
    magp = jnp.exp(lam_re * dt * power)
    lp_re = magp * jnp.cos(lam_im * dt * power)
    lp_im = magp * jnp.sin(lam_im * dt * power)
    return lb_re, lb_im, lp_re, lp_im, bb_re, bb_im


def _even_layer(x, ln_pre, ln_post, w_in, lam_re, lam_im, b_re, b_im, c_re, c_im, d_skip, log_dt,
                w_glu, b_glu, w_alpha_up, b_alpha, gla_norm, w_out):
    bsz, seq, d = x.shape
    groups, state = lam_re.shape
    gs = b_re.shape[-1]
    sw = groups * gs
    rank, kw = w_alpha_up.shape
    vw = gla_norm.shape[0]
    tile = min(EVEN_TILE, seq)
    lg = S5_LANE_GROUPS
    n_blk = groups // lg
    half = lg * state

    lb_re, lb_im, lp_re, lp_im, bb_re, bb_im = _s5_discretize(
        lam_re.astype(F32), lam_im.astype(F32), b_re.astype(F32), b_im.astype(F32),
        log_dt.astype(F32), float(S5_STEPS))

    def lam_pack(re, im):
        out = jnp.zeros((n_blk, SUBLANES, half), F32)
        return out.at[:, 0].set(re.reshape(n_blk, half)).at[:, 1].set(im.reshape(n_blk, half))

    eye = jnp.eye(lg, dtype=F32)

    def b_pack(bb):
        t = bb.reshape(n_blk, lg, state, gs)
        return jnp.einsum('bgpj,gh->bgjhp', t, eye).reshape(n_blk, lg * gs, half)

    def c_pack(cc):
        t = cc.reshape(n_blk, lg, gs, state)
        return jnp.einsum('bgip,gh->bhpgi', t, eye).reshape(n_blk, half, lg * gs)

    bmat = jnp.concatenate([b_pack(bb_re), b_pack(bb_im)], axis=2).astype(BF16)
    cmat = jnp.concatenate([c_pack(c_re.astype(F32)), -c_pack(c_im.astype(F32))], axis=1).astype(BF16)
    main = sw + 2 * kw + 2 * vw
    wa = jnp.zeros((d, LANES), F32).at[:, :rank].set(w_in[:, main:main + rank]).astype(BF16)
    wup = jnp.zeros((LANES, kw), F32).at[:rank].set(w_alpha_up).astype(BF16)
    args = (ln_pre.reshape(1, d), ln_post.reshape(1, d), w_in[:, :main].astype(BF16), wa, wup,
            b_alpha.reshape(1, kw), gla_norm.reshape(1, vw),
            lam_pack(lb_re, lb_im), lam_pack(lp_re, lp_im), bmat, cmat,
            d_skip.reshape(1, sw), w_glu.astype(BF16), b_glu.reshape(1, sw), w_out.astype(BF16))
    rows = tile // S5_STEPS
    scratch = [
        pltpu.VMEM((tile, d), BF16),
        pltpu.VMEM((n_blk, tile, LANES), F32),
        pltpu.VMEM((tile, sw), F32),
        pltpu.VMEM((tile, 2 * half), F32),
        pltpu.VMEM((tile, 2 * half), F32),
        pltpu.VMEM((tile, sw), F32),
        pltpu.VMEM((rows, 2 * half), F32),
        pltpu.VMEM((rows, 2 * half), F32),
        pltpu.VMEM((tile, sw), BF16),
        pltpu.VMEM((n_blk, SUBLANES, half), F32),
        pltpu.VMEM((tile, vw), F32),
        pltpu.VMEM((GLA_HEADS, vw // GLA_HEADS, kw // GLA_HEADS), F32),
    ]
    return pl.pallas_call(
        _even_kernel,
        grid=(bsz, seq // tile),
        in_specs=[_tile_spec(tile, d)] + [_full_spec(a) for a in args],
        out_specs=_tile_spec(tile, d),
        out_shape=jax.ShapeDtypeStruct(x.shape, x.dtype),
        scratch_shapes=scratch,
        compiler_params=_compiler_params(),
        name="even_mixer",
    )(x, *args)


def kernel(x, ln_mix_pre, ln_mix_post, ln_ffn_pre, ln_ffn_post, ev_w_in, s5_lambda_re, s5_lambda_im, s5_b_re, s5_b_im, s5_c_re, s5_c_im, s5_d, s5_log_dt, s5_w_glu, s5_b_glu, gla_w_alpha_up, gla_b_alpha, gla_norm, ev_w_out, od_w_in, ml_conv_w, ml_conv_b, ml_w_q, ml_w_k, ml_w_v, ml_w_gate, ml_b_gate, ml_norm, ml_skip, od_w_out, ffn_w_up, ffn_conv_w, ffn_conv_b, ffn_w_down):
    depth = ln_mix_pre.shape[0]
    for layer in range(depth):
        if layer % 2 == 0:
            e = layer // 2
            x = _even_layer(x, ln_mix_pre[layer], ln_mix_post[layer], ev_w_in[e], s5_lambda_re[e],
                            s5_lambda_im[e], s5_b_re[e], s5_b_im[e], s5_c_re[e], s5_c_im[e], s5_d[e],
                            s5_log_dt[e], s5_w_glu[e], s5_b_glu[e], gla_w_alpha_up[e], gla_b_alpha[e],
                            gla_norm[e], ev_w_out[e])
        else:
            o = layer // 2
            x = _odd_layer(x, ln_mix_pre[layer], ln_mix_post[layer], od_w_in[o], ml_conv_w[o],
                           ml_conv_b[o], ml_w_q[o], ml_w_k[o], ml_w_v[o], ml_w_gate[o], ml_b_gate[o],
                           ml_norm[o], ml_skip[o], od_w_out[o])
        x = _ffn_layer(x, ln_ffn_pre[layer], ln_ffn_post[layer], ffn_w_up[layer], ffn_conv_w[layer],
                       ffn_conv_b[layer], ffn_w_down[layer])
    return x


def _odd_kernel(x_ref, lnpre_ref, lnpost_ref, win_ref, cw_ref, cb_ref, wqk_ref, wv_ref, wg_ref, bg_ref,
                mnorm_ref, skip_ref, wout_ref,
                o_ref,
                h_ref, xm_ref, xc_ref, q_ref, k_ref, v_ref, hout_ref, tail_ref, c_ref, n_ref, m_ref):
    tile = x_ref.shape[0]
    inner = xm_ref.shape[1]
    heads = c_ref.shape[0]
    dh = inner // heads
    n_blocks = wqk_ref.shape[0]
    bw = wqk_ref.shape[1]
    taps = 4

    @pl.when(pl.program_id(1) == 0)
    def _():
        tail_ref[...] = jnp.zeros_like(tail_ref)
        c_ref[...] = jnp.zeros_like(c_ref)
        n_ref[...] = jnp.zeros_like(n_ref)
        m_ref[...] = jnp.zeros_like(m_ref)

    x = x_ref[...]
    h_ref[...] = _rms(x, lnpre_ref[...]).astype(BF16)

    def conv(xm, w):
        out = w[taps - 1:taps] * xm
        for kk in range(taps - 1):
            out = out + w[kk:kk + 1] * pltpu.roll(xm, taps - 1 - kk, axis=0)
        return out

    for blk in range(n_blocks):
        ls = slice(blk * bw, (blk + 1) * bw)
        xm = _dot(h_ref[...], win_ref[:, ls])
        w = cw_ref[:, ls]
        b = cb_ref[:, ls]
        pre = conv(xm, w) + b
        head = jnp.concatenate([tail_ref[:, ls], xm[0:SUBLANES]], axis=0)
        pre_head = (conv(head, w) + b)[SUBLANES:2 * SUBLANES]
        tail_ref[:, ls] = xm[tile - SUBLANES:tile]
        xc = jax.nn.silu(jnp.concatenate([pre_head, pre[SUBLANES:]], axis=0))
        xc_ref[:, ls] = xc
        qk = _dot(xc.astype(BF16), wqk_ref[blk])
        q_ref[:, ls] = qk[:, :bw].astype(BF16)
        k_ref[:, ls] = qk[:, bw:].astype(BF16)
        v_ref[:, ls] = _dot(xm.astype(BF16), wv_ref[blk]).astype(BF16)

    gates = (_dot(q_ref[...], wg_ref[0]) + _dot(k_ref[...], wg_ref[1]) + _dot(v_ref[...], wg_ref[2])
             + bg_ref[...])
    log_f = _log_sigmoid(gates)
    row = lax.broadcasted_iota(jnp.int32, (tile, tile), 0)
    col = lax.broadcasted_iota(jnp.int32, (tile, tile), 1)
    causal = col <= row
    f_cum = _split_dot(jnp.where(causal, 1.0, 0.0).astype(BF16), log_f)
    gates_t = jnp.transpose(gates)
    f_cum_t = jnp.transpose(f_cum)
    scale = dh ** -0.5

    for hd in range(heads):
        hs = slice(hd * dh, (hd + 1) * dh)
        f_col = f_cum[:, heads + hd:heads + hd + 1]
        f_row = f_cum_t[heads + hd:heads + hd + 1, :]
        i_col = gates[:, hd:hd + 1]
        i_row = gates_t[hd:hd + 1, :]
        m_prev = m_ref[hd:hd + 1, 0:1]
        d_log = jnp.where(causal, f_col - f_row + i_row, -jnp.inf)
        inter_log = f_col + m_prev
        m_loc = jnp.maximum(inter_log, jnp.max(d_log, axis=-1, keepdims=True))
        qh = q_ref[:, hs]
        kh = k_ref[:, hs]
        vh = v_ref[:, hs]
        s = _dot_nt(qh, kh) * (scale * jnp.exp(d_log - m_loc))
        w_inter = jnp.exp(inter_log - m_loc)
        c_mat = c_ref[hd]
        num = _dot(s.astype(BF16), vh) + w_inter * _dot(qh, c_mat.astype(BF16))
        n_vec = n_ref[hd:hd + 1, :]
        den = (jnp.sum(s, axis=-1, keepdims=True)
               + w_inter * jnp.sum(qh.astype(F32) * n_vec, axis=-1, keepdims=True))
        denom = jnp.maximum(jnp.abs(den), jnp.exp(-m_loc))
        hout_ref[:, hs] = num * (1.0 / denom)
        f_last = f_col[tile - 1:tile, :]
        w_log = f_last - f_col + i_col
        m_new = jnp.maximum(f_last + m_prev, jnp.max(w_log, axis=0, keepdims=True))
        w_upd = jnp.exp(w_log - m_new)
        decay = jnp.exp(f_last + m_prev - m_new)
        kw = kh.astype(F32) * (scale * w_upd)
        c_ref[hd] = decay * c_mat + _dot_tn(kw.astype(BF16), vh)
        n_ref[hd:hd + 1, :] = decay * n_vec + jnp.sum(kw, axis=0, keepdims=True)
        m_ref[hd:hd + 1, :] = jnp.broadcast_to(m_new, (1, m_ref.shape[1]))

    o_pre = _dot(h_ref[...], win_ref[:, inner:2 * inner])
    for hd in range(heads):
        hs = slice(hd * dh, (hd + 1) * dh)
        hh = hout_ref[:, hs]
        ms = jnp.mean(hh * hh, axis=-1, keepdims=True)
        hn = hh * lax.rsqrt(ms + NORM_EPS) * mnorm_ref[:, hs] + skip_ref[:, hs] * xc_ref[:, hs]
        xm_ref[:, hs] = (jax.nn.sigmoid(o_pre[:, hs]) * hn).astype(BF16)
    mix = _dot(xm_ref[...], wout_ref[...])
    o_ref[...] = x + _rms(mix, lnpost_ref[...])


def _odd_layer(x, ln_pre, ln_post, w_in, conv_w, conv_b, w_q, w_k, w_v, w_gate, b_gate, m_norm, skip,
               w_out):
    bsz, seq, d = x.shape
    inner = w_out.shape[0]
    heads = b_gate.shape[0] // 2
    tile = min(ODD_TILE, seq)
    bw = MXU_DIM
    n_blocks = inner // bw
    per = bw // MLSTM_QKV_BLOCK
    eye = jnp.eye(per, dtype=F32)

    def blockdiag(w):
        t = w.astype(F32).reshape(n_blocks, per, MLSTM_QKV_BLOCK, MLSTM_QKV_BLOCK)
        return jnp.einsum('bncd,nm->bncmd', t, eye).reshape(n_blocks, bw, bw)

    wqk = jnp.concatenate([blockdiag(w_q), blockdiag(w_k)], axis=2).astype(BF16)
    wv = blockdiag(w_v).astype(BF16)
    wg = jnp.zeros((3, inner, LANES), F32).at[:, :, :2 * heads].set(w_gate).astype(BF16)
    bg = jnp.zeros((1, LANES), F32).at[0, :2 * heads].set(b_gate)
    cw = jnp.zeros((SUBLANES, inner), F32).at[:conv_w.shape[0]].set(conv_w)
    args = (ln_pre.reshape(1, d), ln_post.reshape(1, d), w_in.astype(BF16), cw, conv_b.reshape(1, inner),
            wqk, wv, wg, bg, m_norm.reshape(1, inner), skip.reshape(1, inner), w_out.astype(BF16))
    dh = inner // heads
    scratch = [
        pltpu.VMEM((tile, d), BF16),
        pltpu.VMEM((tile, inner), BF16),
        pltpu.VMEM((tile, inner), F32),
        pltpu.VMEM((tile, inner), BF16),
        pltpu.VMEM((tile, inner), BF16),
        pltpu.VMEM((tile, inner), BF16),
        pltpu.VMEM((tile, inner), F32),
        pltpu.VMEM((SUBLANES, inner), F32),
        pltpu.VMEM((heads, dh, dh), F32),
        pltpu.VMEM((SUBLANES, dh), F32),
        pltpu.VMEM((SUBLANES, LANES), F32),
    ]
    return pl.pallas_call(
        _odd_kernel,
        grid=(bsz, seq // tile),
        in_specs=[_tile_spec(tile, d)] + [_full_spec(a) for a in args],
        out_specs=_tile_spec(tile, d),
        out_shape=jax.ShapeDtypeStruct(x.shape, x.dtype),
        scratch_shapes=scratch,
        compiler_params=_compiler_params(),
        name="odd_mixer",
    )(x, *args)
```

```python
import functools
import math

import jax
import jax.numpy as jnp
from jax import lax
from jax.experimental import pallas as pl
from jax.experimental.pallas import tpu as pltpu

F32 = jnp.float32
BF16 = jnp.bfloat16

NORM_EPS = 1e-6
DT_UNUSED = None

S5_GROUP_SIZE = 16
S5_STATE = 64
GLA_HEADS = 4
GLA_CHUNK = 64
GLA_GATE_TEMP = 16.0
MLSTM_HEADS = 4
MLSTM_QKV_BLOCK = 4

LANES = 128
SUBLANES = 8
MXU_DIM = 256
VMEM_LIMIT_BYTES = 58 * 1024 * 1024

EVEN_TILE = 512
S5_STEPS = 32
ODD_TILE = 256
FFN_TILE = 512
FFN_COLS = 256
S5_LANE_GROUPS = LANES // S5_GROUP_SIZE


def _dot(a, b):
    return jnp.dot(a, b, preferred_element_type=F32)


def _dot_nt(a, b):
    return lax.dot_general(a, b, (((1,), (1,)), ((), ())), preferred_element_type=F32)


def _dot_tn(a, b):
    return lax.dot_general(a, b, (((0,), (0,)), ((), ())), preferred_element_type=F32)


def _rms(x, g):
    ms = jnp.mean(x * x, axis=-1, keepdims=True)
    return x * lax.rsqrt(ms + NORM_EPS) * g


def _log_sigmoid(x):
    return jnp.minimum(x, 0.0) - jnp.log1p(jnp.exp(-jnp.abs(x)))


def _split_dot(mask_bf16, x):
    hi = x.astype(BF16)
    lo = (x - hi.astype(F32)).astype(BF16)
    return _dot(mask_bf16, hi) + _dot(mask_bf16, lo)


def _full_spec(arr):
    nd = arr.ndim
    return pl.BlockSpec(arr.shape, lambda b, l, _nd=nd: (0,) * _nd,
                        pipeline_mode=pl.Buffered(1))


def _tile_spec(tile, d):
    return pl.BlockSpec((None, tile, d), lambda b, l: (b, l, 0))


def _compiler_params():
    return pltpu.CompilerParams(dimension_semantics=("arbitrary", "arbitrary"),
                                vmem_limit_bytes=VMEM_LIMIT_BYTES)


def _ffn_kernel(x_ref, lnpre_ref, lnpost_ref, wg_ref, wu_ref, cw_ref, cb_ref, wd_ref,
                o_ref, h_ref, act_ref, tail_ref):
    tile = x_ref.shape[0]
    n_chunks = wg_ref.shape[0]
    fc = wg_ref.shape[2]

    @pl.when(pl.program_id(1) == 0)
    def _():
        tail_ref[...] = jnp.zeros_like(tail_ref)

    x = x_ref[...]
    h_ref[...] = _rms(x, lnpre_ref[...]).astype(BF16)

    def conv3(g, w):
        return (w[0:1] * pltpu.roll(g, 2, axis=0) + w[1:2] * pltpu.roll(g, 1, axis=0)
                + w[2:3] * g)

    def up(c):
        h = h_ref[...]
        return _dot(h, wg_ref[c]), _dot(h, wu_ref[c])

    def gate(c, g, u):
        w = cw_ref[c]
        b = cb_ref[c]
        conv = conv3(g, w) + b
        head = jnp.concatenate([tail_ref[c], g[0:SUBLANES]], axis=0)
        conv_head = (conv3(head, w) + b)[SUBLANES:2 * SUBLANES]
        tail_ref[c] = g[tile - SUBLANES:tile]
        conv = jnp.concatenate([conv_head, conv[SUBLANES:]], axis=0)
        act_ref[:, c * fc:(c + 1) * fc] = (jax.nn.gelu(conv, approximate=True) * u).astype(BF16)

    gu = up(0)
    for c in range(n_chunks):
        gu_next = up(c + 1) if c + 1 < n_chunks else None
        gate(c, *gu)
        gu = gu_next
    o_ref[...] = x + _rms(_dot(act_ref[...], wd_ref[...]), lnpost_ref[...])


def _ffn_layer(x, ln_pre, ln_post, w_up, conv_w, conv_b, w_down):
    bsz, seq, d = x.shape
    f = w_down.shape[0]
    tile = min(FFN_TILE, seq)
    nc = f // FFN_COLS
    wg = w_up[:, :f].reshape(d, nc, FFN_COLS).transpose(1, 0, 2).astype(BF16)
    wu = w_up[:, f:].reshape(d, nc, FFN_COLS).transpose(1, 0, 2).astype(BF16)
    wd = w_down.astype(BF16)
    taps = conv_w.shape[0]
    cw = jnp.zeros((SUBLANES, f), F32).at[:taps].set(conv_w)
    cw = cw.reshape(SUBLANES, nc, FFN_COLS).transpose(1, 0, 2)
    cb = conv_b.reshape(nc, 1, FFN_COLS)
    args = (ln_pre.reshape(1, d), ln_post.reshape(1, d), wg, wu, cw, cb, wd)
    return pl.pallas_call(
        _ffn_kernel,
        grid=(bsz, seq // tile),
        in_specs=[_tile_spec(tile, d)] + [_full_spec(a) for a in args],
        out_specs=_tile_spec(tile, d),
        out_shape=jax.ShapeDtypeStruct(x.shape, x.dtype),
        scratch_shapes=[pltpu.VMEM((tile, d), BF16), pltpu.VMEM((tile, f), BF16),
                        pltpu.VMEM((nc, SUBLANES, FFN_COLS), F32)],
        compiler_params=_compiler_params(),
        name="conv_ffn",
    )(x, *args)


def _even_kernel(x_ref, lnpre_ref, lnpost_ref, wmain_ref, wa_ref, wup_ref, balpha_ref, gnorm_ref,
                 lam_ref, lamc_ref, bmat_ref, cmat_ref, dskip_ref, wglu_ref, bglu_ref, wout_ref,
                 o_ref,
                 h_ref, unat_ref, uperm_ref, bu_ref, xs_ref, yperm_ref, e_ref, cin_ref, anat_ref,
                 s5carry_ref, ogla_ref, glastate_ref):
    tile = x_ref.shape[0]
    n_blk = unat_ref.shape[0]
    sw = n_blk * LANES
    half = bu_ref.shape[2] // 2
    steps = S5_STEPS
    rows = tile // steps
    kw = wup_ref.shape[1]
    vw = ogla_ref.shape[1]
    dk = kw // GLA_HEADS
    dv = vw // GLA_HEADS
    n_chunks = tile // GLA_CHUNK

    @pl.when(pl.program_id(1) == 0)
    def _():
        s5carry_ref[...] = jnp.zeros_like(s5carry_ref)
        glastate_ref[...] = jnp.zeros_like(glastate_ref)

    x = x_ref[...]
    h_ref[...] = _rms(x, lnpre_ref[...]).astype(BF16)
    h = h_ref[...]

    u = _dot(h, wmain_ref[:, 0:sw])
    for blk in range(n_blk):
        unat_ref[blk] = u[:, blk * LANES:(blk + 1) * LANES]
    for blk in range(n_blk):
        for s in range(steps):
            uperm_ref[s * rows:(s + 1) * rows, blk * LANES:(blk + 1) * LANES] = (
                unat_ref[blk, pl.ds(s, rows, stride=steps), :])

    def b_proj(blk):
        bu_ref[blk % 2] = _dot(uperm_ref[:, blk * LANES:(blk + 1) * LANES].astype(BF16),
                               bmat_ref[blk])

    b_proj(0)
    for blk in range(n_blk):
        lanes = slice(blk * LANES, (blk + 1) * LANES)
        slot = blk % 2
        if blk + 1 < n_blk:
            b_proj(blk + 1)
        a_re = lam_ref[blk, 0:1, :]
        a_im = lam_ref[blk, 1:2, :]

        def advance(s, state, store):
            s_re, s_im = state
            b = bu_ref[slot, s * rows:(s + 1) * rows, :]
            n_re = a_re * s_re - a_im * s_im + b[:, :half]
            n_im = a_re * s_im + a_im * s_re + b[:, half:]
            if store:
                xs_ref[slot, s * rows:(s + 1) * rows, :] = jnp.concatenate([n_re, n_im], axis=1)
            return n_re, n_im

        state = (jnp.zeros((rows, half), F32), jnp.zeros((rows, half), F32))
        for s in range(steps):
            state = advance(s, state, False)
        e_ref[...] = jnp.concatenate(state, axis=1)
        ac_re = lamc_ref[blk, 0:1, :]
        ac_im = lamc_ref[blk, 1:2, :]
        c_re = s5carry_ref[blk, 0:1, :]
        c_im = s5carry_ref[blk, 1:2, :]
        for i in range(rows):
            cin_ref[i:i + 1, :] = jnp.concatenate([c_re, c_im], axis=1)
            e_row = e_ref[i:i + 1, :]
            c_re, c_im = (ac_re * c_re - ac_im * c_im + e_row[:, :half],
                          ac_re * c_im + ac_im * c_re + e_row[:, half:])
        s5carry_ref[blk, 0:1, :] = c_re
        s5carry_ref[blk, 1:2, :] = c_im
        cin = cin_ref[...]
        state = (cin[:, :half], cin[:, half:])
        for s in range(steps):
            state = advance(s, state, True)
        yperm_ref[:, lanes] = _dot(xs_ref[slot].astype(BF16), cmat_ref[blk])

    y = yperm_ref[...] + dskip_ref[...] * uperm_ref[...]
    y = jax.nn.gelu(y, approximate=True)
    a_perm = y * jax.nn.sigmoid(_dot(y.astype(BF16), wglu_ref[...]) + bglu_ref[...])
    r_nat = lax.broadcasted_iota(jnp.int32, (tile, tile), 0)
    c_perm = lax.broadcasted_iota(jnp.int32, (tile, tile), 1)
    assert steps & (steps - 1) == 0
    src = jnp.bitwise_and(r_nat, steps - 1) * rows + jnp.right_shift(r_nat, steps.bit_length() - 1)
    unperm = jnp.where(c_perm == src, 1.0, 0.0).astype(BF16)
    anat_ref[...] = _dot(unperm, a_perm.astype(BF16)).astype(BF16)

    o0 = sw
    q = _dot(h, wmain_ref[:, o0:o0 + kw]) * (dk ** -0.5)
    k = _dot(h, wmain_ref[:, o0 + kw:o0 + 2 * kw])
    v = _dot(h, wmain_ref[:, o0 + 2 * kw:o0 + 2 * kw + vw]).astype(BF16)
    a_lr = _dot(h, wa_ref[...])
    z = _dot(a_lr.astype(BF16), wup_ref[...]) + balpha_ref[...]
    log_a = _log_sigmoid(z) * (1.0 / GLA_GATE_TEMP)

    row = lax.broadcasted_iota(jnp.int32, (tile, tile), 0)
    col = lax.broadcasted_iota(jnp.int32, (tile, tile), 1)
    chunk_start = jnp.bitwise_and(row, -GLA_CHUNK)
    in_chunk = jnp.where(col >= chunk_start, jnp.where(col < chunk_start + GLA_CHUNK, 1.0, 0.0), 0.0)
    causal = jnp.where(col <= row, in_chunk, 0.0)
    g = _split_dot(causal.astype(BF16), log_a)
    g_last = _split_dot(in_chunk.astype(BF16), log_a)
    q_dec = (q * jnp.exp(g)).astype(BF16)
    k_dec = (k * jnp.exp(-g)).astype(BF16)
    k_upd = (k * jnp.exp(g_last - g)).astype(BF16)
    decay = jnp.exp(g_last)
    keep = causal > 0.5

    for hd in range(GLA_HEADS):
        ks = slice(hd * dk, (hd + 1) * dk)
        vs = slice(hd * dv, (hd + 1) * dv)
        qh, kh, kuh, vh = q_dec[:, ks], k_dec[:, ks], k_upd[:, ks], v[:, vs]
        scores = jnp.where(keep, _dot_nt(qh, kh), 0.0).astype(BF16)
        o_intra = _dot(scores, vh)
        chunks = [slice(c * GLA_CHUNK, (c + 1) * GLA_CHUNK) for c in range(n_chunks)]
        d_st = [_dot_tn(vh[rs], kuh[rs]) for rs in chunks]
        st = glastate_ref[hd]
        states = []
        for c in range(n_chunks):
            states.append(st.astype(BF16))
            st = st * decay[c * GLA_CHUNK:c * GLA_CHUNK + 1, ks] + d_st[c]
        glastate_ref[hd] = st
        for c, rs in enumerate(chunks):
            ogla_ref[rs, vs] = o_intra[rs] + _dot_nt(qh[rs], states[c])

    r = _dot(h, wmain_ref[:, o0 + 2 * kw + vw:o0 + 2 * kw + 2 * vw])
    gate = jax.nn.silu(r)
    for hd in range(GLA_HEADS):
        vs = slice(hd * dv, (hd + 1) * dv)
        oh = ogla_ref[:, vs]
        ms = jnp.mean(oh * oh, axis=-1, keepdims=True)
        ogla_ref[:, vs] = oh * lax.rsqrt(ms + NORM_EPS) * gnorm_ref[:, vs] * gate[:, vs]

    mix = (_dot(anat_ref[...], wout_ref[0:sw, :])
           + _dot(ogla_ref[...].astype(BF16), wout_ref[sw:sw + vw, :]))
    o_ref[...] = x + _rms(mix, lnpost_ref[...])


def _s5_discretize(lam_re, lam_im, b_re, b_im, log_dt, power):
    dt = jnp.exp(log_dt)[:, None]
    mag = jnp.exp(lam_re * dt)
    lb_re = mag * jnp.cos(lam_im * dt)
    lb_im = mag * jnp.sin(lam_im * dt)
    inv = 1.0 / (lam_re * lam_re + lam_im * lam_im)
    zr = ((lb_re - 1.0) * lam_re + lb_im * lam_im) * inv
    zi = (lb_im * lam_re - (lb_re - 1.0) * lam_im) * inv
    bb_re = zr[..., None] * b_re - zi[..., None] * b_im
    bb_im = zr[..., None] * b_im + zi[..., None] * b_re
    magp = jnp.exp(lam_re * dt * power)
    lp_re = magp * jnp.cos(lam_im * dt * power)
    lp_im = magp * jnp.sin(lam_im * dt * power)
    return lb_re, lb_im, lp_re, lp_im, bb_re, bb_im


def _even_layer(x, ln_pre, ln_post, w_in, lam_re, lam_im, b_re, b_im, c_re, c_im, d_skip, log_dt,
                w_glu, b_glu, w_alpha_up, b_alpha, gla_norm, w_out):
    bsz, seq, d = x.shape
    groups, state = lam_re.shape
    gs = b_re.shape[-1]
    sw = groups * gs
    rank, kw = w_alpha_up.shape
    vw = gla_norm.shape[0]
    tile = min(EVEN_TILE, seq)
    lg = S5_LANE_GROUPS
    n_blk = groups // lg
    half = lg * state

    lb_re, lb_im, lp_re, lp_im, bb_re, bb_im = _s5_discretize(
        lam_re.astype(F32), lam_im.astype(F32), b_re.astype(F32), b_im.astype(F32),
        log_dt.astype(F32), float(S5_STEPS))

    def lam_pack(re, im):
        out = jnp.zeros((n_blk, SUBLANES, half), F32)
        return out.at[:, 0].set(re.reshape(n_blk, half)).at[:, 1].set(im.reshape(n_blk, half))

    eye = jnp.eye(lg, dtype=F32)

    def b_pack(bb):
        t = bb.reshape(n_blk, lg, state, gs)
        return jnp.einsum('bgpj,gh->bgjhp', t, eye).reshape(n_blk, lg * gs, half)

    def c_pack(cc):
        t = cc.reshape(n_blk, lg, gs, state)
        return jnp.einsum('bgip,gh->bhpgi', t, eye).reshape(n_blk, half, lg * gs)

    bmat = jnp.concatenate([b_pack(bb_re), b_pack(bb_im)], axis=2).astype(BF16)
    cmat = jnp.concatenate([c_pack(c_re.astype(F32)), -c_pack(c_im.astype(F32))], axis=1).astype(BF16)
    main = sw + 2 * kw + 2 * vw
    wa = jnp.zeros((d, LANES), F32).at[:, :rank].set(w_in[:, main:main + rank]).astype(BF16)
    wup = jnp.zeros((LANES, kw), F32).at[:rank].set(w_alpha_up).astype(BF16)
    args = (ln_pre.reshape(1, d), ln_post.reshape(1, d), w_in[:, :main].astype(BF16), wa, wup,
            b_alpha.reshape(1, kw), gla_norm.reshape(1, vw),
            lam_pack(lb_re, lb_im), lam_pack(lp_re, lp_im), bmat, cmat,
            d_skip.reshape(1, sw), w_glu.astype(BF16), b_glu.reshape(1, sw), w_out.astype(BF16))
    rows = tile // S5_STEPS
    scratch = [
        pltpu.VMEM((tile, d), BF16),
        pltpu.VMEM((n_blk, tile, LANES), F32),
        pltpu.VMEM((tile, sw), F32),
        pltpu.VMEM((2, tile, 2 * half), F32),
        pltpu.VMEM((2, tile, 2 * half), F32),
        pltpu.VMEM((tile, sw), F32),
        pltpu.VMEM((rows, 2 * half), F32),
        pltpu.VMEM((rows, 2 * half), F32),
        pltpu.VMEM((tile, sw), BF16),
        pltpu.VMEM((n_blk, SUBLANES, half), F32),
        pltpu.VMEM((tile, vw), F32),
        pltpu.VMEM((GLA_HEADS, vw // GLA_HEADS, kw // GLA_HEADS), F32),
    ]
    return pl.pallas_call(
        _even_kernel,
        grid=(bsz, seq // tile),
        in_specs=[_tile_spec(tile, d)] + [_full_spec(a) for a in args],
        out_specs=_tile_spec(tile, d),
        out_shape=jax.ShapeDtypeStruct(x.shape, x.dtype),
        scratch_shapes=scratch,
        compiler_params=_compiler_params(),
        name="even_mixer",
    )(x, *args)


def kernel(x, ln_mix_pre, ln_mix_post, ln_ffn_pre, ln_ffn_post, ev_w_in, s5_lambda_re, s5_lambda_im, s5_b_re, s5_b_im, s5_c_re, s5_c_im, s5_d, s5_log_dt, s5_w_glu, s5_b_glu, gla_w_alpha_up, gla_b_alpha, gla_norm, ev_w_out, od_w_in, ml_conv_w, ml_conv_b, ml_w_q, ml_w_k, ml_w_v, ml_w_gate, ml_b_gate, ml_norm, ml_skip, od_w_out, ffn_w_up, ffn_conv_w, ffn_conv_b, ffn_w_down):
    depth = ln_mix_pre.shape[0]
    for layer in range(depth):
        if layer % 2 == 0:
            e = layer // 2
            x = _even_layer(x, ln_mix_pre[layer], ln_mix_post[layer], ev_w_in[e], s5_lambda_re[e],
                            s5_lambda_im[e], s5_b_re[e], s5_b_im[e], s5_c_re[e], s5_c_im[e], s5_d[e],
                            s5_log_dt[e], s5_w_glu[e], s5_b_glu[e], gla_w_alpha_up[e], gla_b_alpha[e],
                            gla_norm[e], ev_w_out[e])
        else:
            o = layer // 2
            x = _odd_layer(x, ln_mix_pre[layer], ln_mix_post[layer], od_w_in[o], ml_conv_w[o],
                           ml_conv_b[o], ml_w_q[o], ml_w_k[o], ml_w_v[o], ml_w_gate[o], ml_b_gate[o],
                           ml_norm[o], ml_skip[o], od_w_out[o])
        x = _ffn_layer(x, ln_ffn_pre[layer], ln_ffn_post[layer], ffn_w_up[layer], ffn_conv_w[layer],
                       ffn_conv_b[layer], ffn_w_down[layer])
    return x


def _odd_kernel(x_ref, lnpre_ref, lnpost_ref, win_ref, cw_ref, cb_ref, wqk_ref, wv_ref, wg_ref, bg_ref,
                mnorm_ref, skip_ref, wout_ref,
                o_ref,
                h_ref, xm_ref, xc_ref, q_ref, k_ref, v_ref, hout_ref, opre_ref, tail_ref, c_ref, n_ref,
                m_ref):
    tile = x_ref.shape[0]
    inner = xm_ref.shape[1]
    heads = c_ref.shape[0]
    dh = inner // heads
    n_blocks = wqk_ref.shape[0]
    bw = wqk_ref.shape[1]
    taps = 4

    @pl.when(pl.program_id(1) == 0)
    def _():
        tail_ref[...] = jnp.zeros_like(tail_ref)
        c_ref[...] = jnp.zeros_like(c_ref)
        n_ref[...] = jnp.zeros_like(n_ref)
        m_ref[...] = jnp.zeros_like(m_ref)

    x = x_ref[...]
    h_ref[...] = _rms(x, lnpre_ref[...]).astype(BF16)

    def conv(xm, w):
        out = w[taps - 1:taps] * xm
        for kk in range(taps - 1):
            out = out + w[kk:kk + 1] * pltpu.roll(xm, taps - 1 - kk, axis=0)
        return out

    def in_proj(blk):
        return _dot(h_ref[...], win_ref[:, blk * bw:(blk + 1) * bw])

    xm_next = in_proj(0)
    for blk in range(n_blocks):
        ls = slice(blk * bw, (blk + 1) * bw)
        xm = xm_next
        v_ref[:, ls] = _dot(xm.astype(BF16), wv_ref[blk]).astype(BF16)
        if blk + 1 < n_blocks:
            xm_next = in_proj(blk + 1)
        w = cw_ref[:, ls]
        b = cb_ref[:, ls]
        pre = conv(xm, w) + b
        head = jnp.concatenate([tail_ref[:, ls], xm[0:SUBLANES]], axis=0)
        pre_head = (conv(head, w) + b)[SUBLANES:2 * SUBLANES]
        tail_ref[:, ls] = xm[tile - SUBLANES:tile]
        xc = jax.nn.silu(jnp.concatenate([pre_head, pre[SUBLANES:]], axis=0))
        xc_ref[:, ls] = xc
        qk = _dot(xc.astype(BF16), wqk_ref[blk])
        q_ref[:, ls] = qk[:, :bw].astype(BF16)
        k_ref[:, ls] = qk[:, bw:].astype(BF16)

    gates = (_dot(q_ref[...], wg_ref[0]) + _dot(k_ref[...], wg_ref[1]) + _dot(v_ref[...], wg_ref[2])
             + bg_ref[...])
    log_f = _log_sigmoid(gates)
    row = lax.broadcasted_iota(jnp.int32, (tile, tile), 0)
    col = lax.broadcasted_iota(jnp.int32, (tile, tile), 1)
    causal = col <= row
    f_cum = _split_dot(jnp.where(causal, 1.0, 0.0).astype(BF16), log_f)
    gates_t = jnp.transpose(gates)
    f_cum_t = jnp.transpose(f_cum)
    scale = dh ** -0.5

    def state_matmuls(hd):
        hs = slice(hd * dh, (hd + 1) * dh)
        qh = q_ref[:, hs]
        c_mat = c_ref[hd]
        return _dot_nt(qh, k_ref[:, hs]), _dot(qh, c_mat.astype(BF16)), c_mat

    ahead = state_matmuls(0)
    for hd in range(heads):
        qk_raw, inter_raw, c_mat = ahead
        if hd + 1 < heads:
            ahead = state_matmuls(hd + 1)
        hs = slice(hd * dh, (hd + 1) * dh)
        f_col = f_cum[:, heads + hd:heads + hd + 1]
        f_row = f_cum_t[heads + hd:heads + hd + 1, :]
        i_col = gates[:, hd:hd + 1]
        i_row = gates_t[hd:hd + 1, :]
        m_prev = m_ref[hd:hd + 1, 0:1]
        d_log = jnp.where(causal, f_col - f_row + i_row, -jnp.inf)
        inter_log = f_col + m_prev
        m_loc = jnp.maximum(inter_log, jnp.max(d_log, axis=-1, keepdims=True))
        qh = q_ref[:, hs]
        kh = k_ref[:, hs]
        vh = v_ref[:, hs]
        s = qk_raw * (scale * jnp.exp(d_log - m_loc))
        w_inter = jnp.exp(inter_log - m_loc)
        num = _dot(s.astype(BF16), vh) + w_inter * inter_raw
        n_vec = n_ref[hd:hd + 1, :]
        den = (jnp.sum(s, axis=-1, keepdims=True)
               + w_inter * jnp.sum(qh.astype(F32) * n_vec, axis=-1, keepdims=True))
        denom = jnp.maximum(jnp.abs(den), jnp.exp(-m_loc))
        hout_ref[:, hs] = num * (1.0 / denom)
        f_last = f_col[tile - 1:tile, :]
        w_log = f_last - f_col + i_col
        m_new = jnp.maximum(f_last + m_prev, jnp.max(w_log, axis=0, keepdims=True))
        w_upd = jnp.exp(w_log - m_new)
        decay = jnp.exp(f_last + m_prev - m_new)
        kw = kh.astype(F32) * (scale * w_upd)
        c_ref[hd] = decay * c_mat + _dot_tn(kw.astype(BF16), vh)
        n_ref[hd:hd + 1, :] = decay * n_vec + jnp.sum(kw, axis=0, keepdims=True)
        m_ref[hd:hd + 1, :] = jnp.broadcast_to(m_new, (1, m_ref.shape[1]))
        opre_ref[:, hs] = _dot(h_ref[...], win_ref[:, inner + hd * dh:inner + (hd + 1) * dh])

    for hd in range(heads):
        hs = slice(hd * dh, (hd + 1) * dh)
        hh = hout_ref[:, hs]
        ms = jnp.mean(hh * hh, axis=-1, keepdims=True)
        hn = hh * lax.rsqrt(ms + NORM_EPS) * mnorm_ref[:, hs] + skip_ref[:, hs] * xc_ref[:, hs]
        xm_ref[:, hs] = (jax.nn.sigmoid(opre_ref[:, hs]) * hn).astype(BF16)
    mix = _dot(xm_ref[...], wout_ref[...])
    o_ref[...] = x + _rms(mix, lnpost_ref[...])


def _odd_layer(x, ln_pre, ln_post, w_in, conv_w, conv_b, w_q, w_k, w_v, w_gate, b_gate, m_norm, skip,
               w_out):
    bsz, seq, d = x.shape
    inner = w_out.shape[0]
    heads = b_gate.shape[0] // 2
    tile = min(ODD_TILE, seq)
    bw = MXU_DIM
    n_blocks = inner // bw
    per = bw // MLSTM_QKV_BLOCK
    eye = jnp.eye(per, dtype=F32)

    def blockdiag(w):
        t = w.astype(F32).reshape(n_blocks, per, MLSTM_QKV_BLOCK, MLSTM_QKV_BLOCK)
        return jnp.einsum('bncd,nm->bncmd', t, eye).reshape(n_blocks, bw, bw)

    wqk = jnp.concatenate([blockdiag(w_q), blockdiag(w_k)], axis=2).astype(BF16)
    wv = blockdiag(w_v).astype(BF16)
    wg = jnp.zeros((3, inner, LANES), F32).at[:, :, :2 * heads].set(w_gate).astype(BF16)
    bg = jnp.zeros((1, LANES), F32).at[0, :2 * heads].set(b_gate)
    cw = jnp.zeros((SUBLANES, inner), F32).at[:conv_w.shape[0]].set(conv_w)
    args = (ln_pre.reshape(1, d), ln_post.reshape(1, d), w_in.astype(BF16), cw, conv_b.reshape(1, inner),
            wqk, wv, wg, bg, m_norm.reshape(1, inner), skip.reshape(1, inner), w_out.astype(BF16))
    dh = inner // heads
    scratch = [
        pltpu.VMEM((tile, d), BF16),
        pltpu.VMEM((tile, inner), BF16),
        pltpu.VMEM((tile, inner), F32),
        pltpu.VMEM((tile, inner), BF16),
        pltpu.VMEM((tile, inner), BF16),
        pltpu.VMEM((tile, inner), BF16),
        pltpu.VMEM((tile, inner), F32),
        pltpu.VMEM((tile, inner), F32),
        pltpu.VMEM((SUBLANES, inner), F32),
        pltpu.VMEM((heads, dh, dh), F32),
        pltpu.VMEM((SUBLANES, dh), F32),
        pltpu.VMEM((SUBLANES, LANES), F32),
    ]
    return pl.pallas_call(
        _odd_kernel,
        grid=(bsz, seq // tile),
        in_specs=[_tile_spec(tile, d)] + [_full_spec(a) for a in args],
        out_specs=_tile_spec(tile, d),
        out_shape=jax.ShapeDtypeStruct(x.shape, x.dtype),
        scratch_shapes=scratch,
        compiler_params=_compiler_params(),
        name="odd_mixer",
    )(x, *args)
```

```python
import functools
import math

import jax
import jax.numpy as jnp
from jax import lax
from jax.experimental import pallas as pl
from jax.experimental.pallas import tpu as pltpu

F32 = jnp.float32
BF16 = jnp.bfloat16

NORM_EPS = 1e-6
DT_UNUSED = None

S5_GROUP_SIZE = 16
S5_STATE = 64
GLA_HEADS = 4
GLA_CHUNK = 64
GLA_GATE_TEMP = 16.0
MLSTM_HEADS = 4
MLSTM_QKV_BLOCK = 4

LANES = 128
SUBLANES = 8
MXU_DIM = 256
VMEM_LIMIT_BYTES = 58 * 1024 * 1024

EVEN_TILE = 512
S5_STEPS = 32
ODD_TILE = 256
FFN_TILE = 512
FFN_COLS = 256
S5_LANE_GROUPS = LANES // S5_GROUP_SIZE


def _dot(a, b):
    return jnp.dot(a, b, preferred_element_type=F32)


def _dot_nt(a, b):
    return lax.dot_general(a, b, (((1,), (1,)), ((), ())), preferred_element_type=F32)


def _dot_tn(a, b):
    return lax.dot_general(a, b, (((0,), (0,)), ((), ())), preferred_element_type=F32)


def _rms(x, g):
    ms = jnp.mean(x * x, axis=-1, keepdims=True)
    return x * lax.rsqrt(ms + NORM_EPS) * g


def _log_sigmoid(x):
    return jnp.minimum(x, 0.0) - jnp.log1p(jnp.exp(-jnp.abs(x)))


def _split_dot(mask_bf16, x):
    hi = x.astype(BF16)
    lo = (x - hi.astype(F32)).astype(BF16)
    return _dot(mask_bf16, hi) + _dot(mask_bf16, lo)


def _full_spec(arr):
    nd = arr.ndim
    return pl.BlockSpec(arr.shape, lambda b, l, _nd=nd: (0,) * _nd,
                        pipeline_mode=pl.Buffered(1))


def _tile_spec(tile, d):
    return pl.BlockSpec((None, tile, d), lambda b, l: (b, l, 0))


def _compiler_params():
    return pltpu.CompilerParams(dimension_semantics=("arbitrary", "arbitrary"),
                                vmem_limit_bytes=VMEM_LIMIT_BYTES)


def _ffn_kernel(x_ref, lnpre_ref, lnpost_ref, wg_ref, wu_ref, cw_ref, cb_ref, wd_ref,
                o_ref, h_ref, act_ref, tail_ref):
    tile = x_ref.shape[0]
    n_chunks = wg_ref.shape[0]
    fc = wg_ref.shape[2]

    @pl.when(pl.program_id(1) == 0)
    def _():
        tail_ref[...] = jnp.zeros_like(tail_ref)

    x = x_ref[...]
    h_ref[...] = _rms(x, lnpre_ref[...]).astype(BF16)

    def conv3(g, w):
        return (w[0:1] * pltpu.roll(g, 2, axis=0) + w[1:2] * pltpu.roll(g, 1, axis=0)
                + w[2:3] * g)

    def up(c):
        h = h_ref[...]
        return _dot(h, wg_ref[c]), _dot(h, wu_ref[c])

    def gate(c, g, u):
        w = cw_ref[c]
        b = cb_ref[c]
        conv = conv3(g, w) + b
        head = jnp.concatenate([tail_ref[c], g[0:SUBLANES]], axis=0)
        conv_head = (conv3(head, w) + b)[SUBLANES:2 * SUBLANES]
        tail_ref[c] = g[tile - SUBLANES:tile]
        conv = jnp.concatenate([conv_head, conv[SUBLANES:]], axis=0)
        act_ref[:, c * fc:(c + 1) * fc] = (jax.nn.gelu(conv, approximate=True) * u).astype(BF16)

    gu = up(0)
    for c in range(n_chunks):
        gu_next = up(c + 1) if c + 1 < n_chunks else None
        gate(c, *gu)
        gu = gu_next
    o_ref[...] = x + _rms(_dot(act_ref[...], wd_ref[...]), lnpost_ref[...])


def _ffn_layer(x, ln_pre, ln_post, w_up, conv_w, conv_b, w_down):
    bsz, seq, d = x.shape
    f = w_down.shape[0]
    tile = min(FFN_TILE, seq)
    nc = f // FFN_COLS
    wg = w_up[:, :f].reshape(d, nc, FFN_COLS).transpose(1, 0, 2).astype(BF16)
    wu = w_up[:, f:].reshape(d, nc, FFN_COLS).transpose(1, 0, 2).astype(BF16)
    wd = w_down.astype(BF16)
    taps = conv_w.shape[0]
    cw = jnp.zeros((SUBLANES, f), F32).at[:taps].set(conv_w)
    cw = cw.reshape(SUBLANES, nc, FFN_COLS).transpose(1, 0, 2)
    cb = conv_b.reshape(nc, 1, FFN_COLS)
    args = (ln_pre.reshape(1, d), ln_post.reshape(1, d), wg, wu, cw, cb, wd)
    return pl.pallas_call(
        _ffn_kernel,
        grid=(bsz, seq // tile),
        in_specs=[_tile_spec(tile, d)] + [_full_spec(a) for a in args],
        out_specs=_tile_spec(tile, d),
        out_shape=jax.ShapeDtypeStruct(x.shape, x.dtype),
        scratch_shapes=[pltpu.VMEM((tile, d), BF16), pltpu.VMEM((tile, f), BF16),
                        pltpu.VMEM((nc, SUBLANES, FFN_COLS), F32)],
        compiler_params=_compiler_params(),
        name="conv_ffn",
    )(x, *args)


def _even_kernel(x_ref, lnpre_ref, lnpost_ref, wmain_ref, wa_ref, wup_ref, balpha_ref, gnorm_ref,
                 lam_ref, lamc_ref, bmat_ref, cmat_ref, dskip_ref, wglu_ref, bglu_ref, wout_ref,
                 o_ref,
                 h_ref, unat_ref, uperm_ref, bu_ref, xs_ref, yperm_ref, e_ref, cin_ref, anat_ref,
                 s5carry_ref, ogla_ref, glastate_ref):
    tile = x_ref.shape[0]
    n_blk = unat_ref.shape[0]
    sw = n_blk * LANES
    half = bu_ref.shape[2] // 2
    steps = S5_STEPS
    rows = tile // steps
    kw = wup_ref.shape[1]
    vw = ogla_ref.shape[1]
    dk = kw // GLA_HEADS
    dv = vw // GLA_HEADS
    n_chunks = tile // GLA_CHUNK

    @pl.when(pl.program_id(1) == 0)
    def _():
        s5carry_ref[...] = jnp.zeros_like(s5carry_ref)
        glastate_ref[...] = jnp.zeros_like(glastate_ref)

    x = x_ref[...]
    h_ref[...] = _rms(x, lnpre_ref[...]).astype(BF16)
    h = h_ref[...]

    u = _dot(h, wmain_ref[:, 0:sw])
    for blk in range(n_blk):
        unat_ref[blk] = u[:, blk * LANES:(blk + 1) * LANES]
    for blk in range(n_blk):
        for s in range(steps):
            uperm_ref[s * rows:(s + 1) * rows, blk * LANES:(blk + 1) * LANES] = (
                unat_ref[blk, pl.ds(s, rows, stride=steps), :])

    def b_proj(blk):
        bu_ref[blk % 2] = _dot(uperm_ref[:, blk * LANES:(blk + 1) * LANES].astype(BF16),
                               bmat_ref[blk])

    b_proj(0)
    for blk in range(n_blk):
        lanes = slice(blk * LANES, (blk + 1) * LANES)
        slot = blk % 2
        if blk + 1 < n_blk:
            b_proj(blk + 1)
        a_re = lam_ref[blk, 0:1, :]
        a_im = lam_ref[blk, 1:2, :]

        def advance(s, state, store):
            s_re, s_im = state
            b = bu_ref[slot, s * rows:(s + 1) * rows, :]
            n_re = a_re * s_re - a_im * s_im + b[:, :half]
            n_im = a_re * s_im + a_im * s_re + b[:, half:]
            if store:
                xs_ref[slot, s * rows:(s + 1) * rows, :] = jnp.concatenate([n_re, n_im], axis=1)
            return n_re, n_im

        state = (jnp.zeros((rows, half), F32), jnp.zeros((rows, half), F32))
        for s in range(steps):
            state = advance(s, state, False)
        e_ref[...] = jnp.concatenate(state, axis=1)
        ac_re = lamc_ref[blk, 0:1, :]
        ac_im = lamc_ref[blk, 1:2, :]
        c_re = s5carry_ref[blk, 0:1, :]
        c_im = s5carry_ref[blk, 1:2, :]
        for i in range(rows):
            cin_ref[i:i + 1, :] = jnp.concatenate([c_re, c_im], axis=1)
            e_row = e_ref[i:i + 1, :]
            c_re, c_im = (ac_re * c_re - ac_im * c_im + e_row[:, :half],
                          ac_re * c_im + ac_im * c_re + e_row[:, half:])
        s5carry_ref[blk, 0:1, :] = c_re
        s5carry_ref[blk, 1:2, :] = c_im
        cin = cin_ref[...]
        state = (cin[:, :half], cin[:, half:])
        for s in range(steps):
            state = advance(s, state, True)
        yperm_ref[:, lanes] = _dot(xs_ref[slot].astype(BF16), cmat_ref[blk])

    y = yperm_ref[...] + dskip_ref[...] * uperm_ref[...]
    y = jax.nn.gelu(y, approximate=True)
    a_perm = y * jax.nn.sigmoid(_dot(y.astype(BF16), wglu_ref[...]) + bglu_ref[...])
    r_nat = lax.broadcasted_iota(jnp.int32, (tile, tile), 0)
    c_perm = lax.broadcasted_iota(jnp.int32, (tile, tile), 1)
    assert steps & (steps - 1) == 0
    src = jnp.bitwise_and(r_nat, steps - 1) * rows + jnp.right_shift(r_nat, steps.bit_length() - 1)
    unperm = jnp.where(c_perm == src, 1.0, 0.0).astype(BF16)
    anat_ref[...] = _dot(unperm, a_perm.astype(BF16)).astype(BF16)

    o0 = sw
    q = _dot(h, wmain_ref[:, o0:o0 + kw]) * (dk ** -0.5)
    k = _dot(h, wmain_ref[:, o0 + kw:o0 + 2 * kw])
    v = _dot(h, wmain_ref[:, o0 + 2 * kw:o0 + 2 * kw + vw]).astype(BF16)
    a_lr = _dot(h, wa_ref[...])
    z = _dot(a_lr.astype(BF16), wup_ref[...]) + balpha_ref[...]
    log_a = _log_sigmoid(z) * (1.0 / GLA_GATE_TEMP)

    row = lax.broadcasted_iota(jnp.int32, (tile, tile), 0)
    col = lax.broadcasted_iota(jnp.int32, (tile, tile), 1)
    chunk_start = jnp.bitwise_and(row, -GLA_CHUNK)
    in_chunk = jnp.where(col >= chunk_start, jnp.where(col < chunk_start + GLA_CHUNK, 1.0, 0.0), 0.0)
    causal = jnp.where(col <= row, in_chunk, 0.0)
    g = _split_dot(causal.astype(BF16), log_a)
    g_last = _split_dot(in_chunk.astype(BF16), log_a)
    q_dec = (q * jnp.exp(g)).astype(BF16)
    k_dec = (k * jnp.exp(-g)).astype(BF16)
    k_upd = (k * jnp.exp(g_last - g)).astype(BF16)
    decay = jnp.exp(g_last)
    keep = causal > 0.5

    for hd in range(GLA_HEADS):
        ks = slice(hd * dk, (hd + 1) * dk)
        vs = slice(hd * dv, (hd + 1) * dv)
        qh, kh, kuh, vh = q_dec[:, ks], k_dec[:, ks], k_upd[:, ks], v[:, vs]
        scores = jnp.where(keep, _dot_nt(qh, kh), 0.0).astype(BF16)
        o_intra = _dot(scores, vh)
        chunks = [slice(c * GLA_CHUNK, (c + 1) * GLA_CHUNK) for c in range(n_chunks)]
        d_st = [_dot_tn(vh[rs], kuh[rs]) for rs in chunks]
        st = glastate_ref[hd]
        states = []
        for c in range(n_chunks):
            states.append(st.astype(BF16))
            st = st * decay[c * GLA_CHUNK:c * GLA_CHUNK + 1, ks] + d_st[c]
        glastate_ref[hd] = st
        for c, rs in enumerate(chunks):
            ogla_ref[rs, vs] = o_intra[rs] + _dot_nt(qh[rs], states[c])

    r = _dot(h, wmain_ref[:, o0 + 2 * kw + vw:o0 + 2 * kw + 2 * vw])
    gate = jax.nn.silu(r)
    for hd in range(GLA_HEADS):
        vs = slice(hd * dv, (hd + 1) * dv)
        oh = ogla_ref[:, vs]
        ms = jnp.mean(oh * oh, axis=-1, keepdims=True)
        ogla_ref[:, vs] = oh * lax.rsqrt(ms + NORM_EPS) * gnorm_ref[:, vs] * gate[:, vs]

    mix = (_dot(anat_ref[...], wout_ref[0:sw, :])
           + _dot(ogla_ref[...].astype(BF16), wout_ref[sw:sw + vw, :]))
    o_ref[...] = x + _rms(mix, lnpost_ref[...])


def _s5_discretize(lam_re, lam_im, b_re, b_im, log_dt, power):
    dt = jnp.exp(log_dt)[:, None]
    mag = jnp.exp(lam_re * dt)
    lb_re = mag * jnp.cos(lam_im * dt)
    lb_im = mag * jnp.sin(lam_im * dt)
    inv = 1.0 / (lam_re * lam_re + lam_im * lam_im)
    zr = ((lb_re - 1.0) * lam_re + lb_im * lam_im) * inv
    zi = (lb_im * lam_re - (lb_re - 1.0) * lam_im) * inv
    bb_re = zr[..., None] * b_re - zi[..., None] * b_im
    bb_im = zr[..., None] * b_im + zi[..., None] * b_re
    magp = jnp.exp(lam_re * dt * power)
    lp_re = magp * jnp.cos(lam_im * dt * power)
    lp_im = magp * jnp.sin(lam_im * dt * power)
    return lb_re, lb_im, lp_re, lp_im, bb_re, bb_im


def _even_layer(x, ln_pre, ln_post, w_in, lam_re, lam_im, b_re, b_im, c_re, c_im, d_skip, log_dt,
                w_glu, b_glu, w_alpha_up, b_alpha, gla_norm, w_out):
    bsz, seq, d = x.shape
    groups, state = lam_re.shape
    gs = b_re.shape[-1]
    sw = groups * gs
    rank, kw = w_alpha_up.shape
    vw = gla_norm.shape[0]
    tile = min(EVEN_TILE, seq)
    lg = S5_LANE_GROUPS
    n_blk = groups // lg
    half = lg * state

    lb_re, lb_im, lp_re, lp_im, bb_re, bb_im = _s5_discretize(
        lam_re.astype(F32), lam_im.astype(F32), b_re.astype(F32), b_im.astype(F32),
        log_dt.astype(F32), float(S5_STEPS))

    def lam_pack(re, im):
        out = jnp.zeros((n_blk, SUBLANES, half), F32)
        return out.at[:, 0].set(re.reshape(n_blk, half)).at[:, 1].set(im.reshape(n_blk, half))

    eye = jnp.eye(lg, dtype=F32)

    def b_pack(bb):
        t = bb.reshape(n_blk, lg, state, gs)
        return jnp.einsum('bgpj,gh->bgjhp', t, eye).reshape(n_blk, lg * gs, half)

    def c_pack(cc):
        t = cc.reshape(n_blk, lg, gs, state)
        return jnp.einsum('bgip,gh->bhpgi', t, eye).reshape(n_blk, half, lg * gs)

    bmat = jnp.concatenate([b_pack(bb_re), b_pack(bb_im)], axis=2).astype(BF16)
    cmat = jnp.concatenate([c_pack(c_re.astype(F32)), -c_pack(c_im.astype(F32))], axis=1).astype(BF16)
    main = sw + 2 * kw + 2 * vw
    wa = jnp.zeros((d, LANES), F32).at[:, :rank].set(w_in[:, main:main + rank]).astype(BF16)
    wup = jnp.zeros((LANES, kw), F32).at[:rank].set(w_alpha_up).astype(BF16)
    args = (ln_pre.reshape(1, d), ln_post.reshape(1, d), w_in[:, :main].astype(BF16), wa, wup,
            b_alpha.reshape(1, kw), gla_norm.reshape(1, vw),
            lam_pack(lb_re, lb_im), lam_pack(lp_re, lp_im), bmat, cmat,
            d_skip.reshape(1, sw), w_glu.astype(BF16), b_glu.reshape(1, sw), w_out.astype(BF16))
    rows = tile // S5_STEPS
    scratch = [
        pltpu.VMEM((tile, d), BF16),
        pltpu.VMEM((n_blk, tile, LANES), F32),
        pltpu.VMEM((tile, sw), F32),
        pltpu.VMEM((2, tile, 2 * half), F32),
        pltpu.VMEM((2, tile, 2 * half), F32),
        pltpu.VMEM((tile, sw), F32),
        pltpu.VMEM((rows, 2 * half), F32),
        pltpu.VMEM((rows, 2 * half), F32),
        pltpu.VMEM((tile, sw), BF16),
        pltpu.VMEM((n_blk, SUBLANES, half), F32),
        pltpu.VMEM((tile, vw), F32),
        pltpu.VMEM((GLA_HEADS, vw // GLA_HEADS, kw // GLA_HEADS), F32),
    ]
    return pl.pallas_call(
        _even_kernel,
        grid=(bsz, seq // tile),
        in_specs=[_tile_spec(tile, d)] + [_full_spec(a) for a in args],
        out_specs=_tile_spec(tile, d),
        out_shape=jax.ShapeDtypeStruct(x.shape, x.dtype),
        scratch_shapes=scratch,
        compiler_params=_compiler_params(),
        name="even_mixer",
    )(x, *args)


def kernel(x, ln_mix_pre, ln_mix_post, ln_ffn_pre, ln_ffn_post, ev_w_in, s5_lambda_re, s5_lambda_im, s5_b_re, s5_b_im, s5_c_re, s5_c_im, s5_d, s5_log_dt, s5_w_glu, s5_b_glu, gla_w_alpha_up, gla_b_alpha, gla_norm, ev_w_out, od_w_in, ml_conv_w, ml_conv_b, ml_w_q, ml_w_k, ml_w_v, ml_w_gate, ml_b_gate, ml_norm, ml_skip, od_w_out, ffn_w_up, ffn_conv_w, ffn_conv_b, ffn_w_down):
    depth = ln_mix_pre.shape[0]
    for layer in range(depth):
        if layer % 2 == 0:
            e = layer // 2
            x = _even_layer(x, ln_mix_pre[layer], ln_mix_post[layer], ev_w_in[e], s5_lambda_re[e],
                            s5_lambda_im[e], s5_b_re[e], s5_b_im[e], s5_c_re[e], s5_c_im[e], s5_d[e],
                            s5_log_dt[e], s5_w_glu[e], s5_b_glu[e], gla_w_alpha_up[e], gla_b_alpha[e],
                            gla_norm[e], ev_w_out[e])
        else:
            o = layer // 2
            x = _odd_layer(x, ln_mix_pre[layer], ln_mix_post[layer], od_w_in[o], ml_conv_w[o],
                           ml_conv_b[o], ml_w_q[o], ml_w_k[o], ml_w_v[o], ml_w_gate[o], ml_b_gate[o],
                           ml_norm[o], ml_skip[o], od_w_out[o])
        x = _ffn_layer(x, ln_ffn_pre[layer], ln_ffn_post[layer], ffn_w_up[layer], ffn_conv_w[layer],
                       ffn_conv_b[layer], ffn_w_down[layer])
    return x


def _odd_kernel(x_ref, lnpre_ref, lnpost_ref, win_ref, cw_ref, cb_ref, wqk_ref, wv_ref, wg_ref, bg_ref,
                mnorm_ref, skip_ref, wout_ref,
                o_ref,
                h_ref, xc_ref, xcb_ref, xmb_ref, q_ref, k_ref, v_ref, tail_ref, c_ref, n_ref,
                m_ref):
    tile = x_ref.shape[0]
    inner = xc_ref.shape[1]
    heads = c_ref.shape[0]
    dh = inner // heads
    n_blocks = wqk_ref.shape[0]
    bw = wqk_ref.shape[1]
    taps = 4

    @pl.when(pl.program_id(1) == 0)
    def _():
        tail_ref[...] = jnp.zeros_like(tail_ref)
        c_ref[...] = jnp.zeros_like(c_ref)
        n_ref[...] = jnp.zeros_like(n_ref)
        m_ref[...] = jnp.zeros_like(m_ref)

    x = x_ref[...]
    h_ref[...] = _rms(x, lnpre_ref[...]).astype(BF16)

    def conv(xm, w):
        out = w[taps - 1:taps] * xm
        for kk in range(taps - 1):
            out = out + w[kk:kk + 1] * pltpu.roll(xm, taps - 1 - kk, axis=0)
        return out

    gw = 2 * bw
    n_groups = inner // gw

    def in_proj(grp):
        return _dot(h_ref[...], win_ref[:, grp * gw:(grp + 1) * gw])

    def conv_silu(grp, xm):
        ls = slice(grp * gw, (grp + 1) * gw)
        w = cw_ref[:, ls]
        b = cb_ref[:, ls]
        pre = conv(xm, w) + b
        head = jnp.concatenate([tail_ref[:, ls], xm[0:SUBLANES]], axis=0)
        pre_head = (conv(head, w) + b)[SUBLANES:2 * SUBLANES]
        tail_ref[:, ls] = xm[tile - SUBLANES:tile]
        xc = jax.nn.silu(jnp.concatenate([pre_head, pre[SUBLANES:]], axis=0))
        xc_ref[:, ls] = xc
        xcb_ref[:, ls] = xc.astype(BF16)
        xmb_ref[:, ls] = xm.astype(BF16)

    def headwise(grp):
        for blk in range(grp * (gw // bw), (grp + 1) * (gw // bw)):
            ls = slice(blk * bw, (blk + 1) * bw)
            qk = _dot(xcb_ref[:, ls], wqk_ref[blk])
            q_ref[:, ls] = qk[:, :bw].astype(BF16)
            k_ref[:, ls] = qk[:, bw:].astype(BF16)
            v_ref[:, ls] = _dot(xmb_ref[:, ls], wv_ref[blk]).astype(BF16)

    xm_next = in_proj(0)
    for grp in range(n_groups):
        xm = xm_next
        if grp + 1 < n_groups:
            xm_next = in_proj(grp + 1)
        if grp > 0:
            headwise(grp - 1)
        conv_silu(grp, xm)
    headwise(n_groups - 1)

    gates = (_dot(q_ref[...], wg_ref[0]) + _dot(k_ref[...], wg_ref[1]) + _dot(v_ref[...], wg_ref[2])
             + bg_ref[...])
    log_f = _log_sigmoid(gates)
    row = lax.broadcasted_iota(jnp.int32, (tile, tile), 0)
    col = lax.broadcasted_iota(jnp.int32, (tile, tile), 1)
    causal = col <= row
    f_cum = _split_dot(jnp.where(causal, 1.0, 0.0).astype(BF16), log_f)
    gates_t = jnp.transpose(gates)
    f_cum_t = jnp.transpose(f_cum)
    scale = dh ** -0.5

    def state_matmuls(hd):
        hs = slice(hd * dh, (hd + 1) * dh)
        qh = q_ref[:, hs]
        c_mat = c_ref[hd]
        return _dot_nt(qh, k_ref[:, hs]), _dot(qh, c_mat.astype(BF16)), c_mat

    ahead = state_matmuls(0)
    for hd in range(heads):
        qk_raw, inter_raw, c_mat = ahead
        if hd + 1 < heads:
            ahead = state_matmuls(hd + 1)
        hs = slice(hd * dh, (hd + 1) * dh)
        f_col = f_cum[:, heads + hd:heads + hd + 1]
        f_row = f_cum_t[heads + hd:heads + hd + 1, :]
        i_col = gates[:, hd:hd + 1]
        i_row = gates_t[hd:hd + 1, :]
        m_prev = m_ref[hd:hd + 1, 0:1]
        d_log = jnp.where(causal, f_col - f_row + i_row, -jnp.inf)
        inter_log = f_col + m_prev
        m_loc = jnp.maximum(inter_log, jnp.max(d_log, axis=-1, keepdims=True))
        qh = q_ref[:, hs]
        kh = k_ref[:, hs]
        vh = v_ref[:, hs]
        s = qk_raw * (scale * jnp.exp(d_log - m_loc))
        w_inter = jnp.exp(inter_log - m_loc)
        num = _dot(s.astype(BF16), vh) + w_inter * inter_raw
        o_pre = _dot(h_ref[...], win_ref[:, inner + hd * dh:inner + (hd + 1) * dh])
        n_vec = n_ref[hd:hd + 1, :]
        den = (jnp.sum(s, axis=-1, keepdims=True)
               + w_inter * jnp.sum(qh.astype(F32) * n_vec, axis=-1, keepdims=True))
        denom = jnp.maximum(jnp.abs(den), jnp.exp(-m_loc))
        hh = num * (1.0 / denom)
        ms = jnp.mean(hh * hh, axis=-1, keepdims=True)
        hn = hh * lax.rsqrt(ms + NORM_EPS) * mnorm_ref[:, hs] + skip_ref[:, hs] * xc_ref[:, hs]
        gated = (jax.nn.sigmoid(o_pre) * hn).astype(BF16)
        part = _dot(gated, wout_ref[hs, :])
        mix = part if hd == 0 else mix + part
        f_last = f_col[tile - 1:tile, :]
        w_log = f_last - f_col + i_col
        m_new = jnp.maximum(f_last + m_prev, jnp.max(w_log, axis=0, keepdims=True))
        w_upd = jnp.exp(w_log - m_new)
        decay = jnp.exp(f_last + m_prev - m_new)
        kw = kh.astype(F32) * (scale * w_upd)
        c_ref[hd] = decay * c_mat + _dot_tn(kw.astype(BF16), vh)
        n_ref[hd:hd + 1, :] = decay * n_vec + jnp.sum(kw, axis=0, keepdims=True)
        m_ref[hd:hd + 1, :] = jnp.broadcast_to(m_new, (1, m_ref.shape[1]))

    o_ref[...] = x + _rms(mix, lnpost_ref[...])


def _odd_layer(x, ln_pre, ln_post, w_in, conv_w, conv_b, w_q, w_k, w_v, w_gate, b_gate, m_norm, skip,
               w_out):
    bsz, seq, d = x.shape
    inner = w_out.shape[0]
    heads = b_gate.shape[0] // 2
    tile = min(ODD_TILE, seq)
    bw = MXU_DIM
    n_blocks = inner // bw
    per = bw // MLSTM_QKV_BLOCK
    eye = jnp.eye(per, dtype=F32)

    def blockdiag(w):
        t = w.astype(F32).reshape(n_blocks, per, MLSTM_QKV_BLOCK, MLSTM_QKV_BLOCK)
        return jnp.einsum('bncd,nm->bncmd', t, eye).reshape(n_blocks, bw, bw)

    wqk = jnp.concatenate([blockdiag(w_q), blockdiag(w_k)], axis=2).astype(BF16)
    wv = blockdiag(w_v).astype(BF16)
    wg = jnp.zeros((3, inner, LANES), F32).at[:, :, :2 * heads].set(w_gate).astype(BF16)
    bg = jnp.zeros((1, LANES), F32).at[0, :2 * heads].set(b_gate)
    cw = jnp.zeros((SUBLANES, inner), F32).at[:conv_w.shape[0]].set(conv_w)
    args = (ln_pre.reshape(1, d), ln_post.reshape(1, d), w_in.astype(BF16), cw, conv_b.reshape(1, inner),
            wqk, wv, wg, bg, m_norm.reshape(1, inner), skip.reshape(1, inner), w_out.astype(BF16))
    dh = inner // heads
    scratch = [
        pltpu.VMEM((tile, d), BF16),
        pltpu.VMEM((tile, inner), F32),
        pltpu.VMEM((tile, inner), BF16),
        pltpu.VMEM((tile, inner), BF16),
        pltpu.VMEM((tile, inner), BF16),
        pltpu.VMEM((tile, inner), BF16),
        pltpu.VMEM((tile, inner), BF16),
        pltpu.VMEM((SUBLANES, inner), F32),
        pltpu.VMEM((heads, dh, dh), F32),
        pltpu.VMEM((SUBLANES, dh), F32),
        pltpu.VMEM((SUBLANES, LANES), F32),
    ]
    return pl.pallas_call(
        _odd_kernel,
        grid=(bsz, seq // tile),
        in_specs=[_tile_spec(tile, d)] + [_full_spec(a) for a in args],
        out_specs=_tile_spec(tile, d),
        out_shape=jax.ShapeDtypeStruct(x.shape, x.dtype),
        scratch_shapes=scratch,
        compiler_params=_compiler_params(),
        name="odd_mixer",
    )(x, *args)
```

```python
import functools
import math

import jax
import jax.numpy as jnp
from jax import lax
from jax.experimental import pallas as pl
from jax.experimental.pallas import tpu as pltpu

F32 = jnp.float32
BF16 = jnp.bfloat16

NORM_EPS = 1e-6
DT_UNUSED = None

S5_GROUP_SIZE = 16
S5_STATE = 64
GLA_HEADS = 4
GLA_CHUNK = 64
GLA_GATE_TEMP = 16.0
MLSTM_HEADS = 4
MLSTM_QKV_BLOCK = 4

LANES = 128
SUBLANES = 8
MXU_DIM = 256
VMEM_LIMIT_BYTES = 58 * 1024 * 1024

EVEN_TILE = 512
S5_STEPS = 32
ODD_TILE = 256
FFN_TILE = 512
FFN_COLS = 256
S5_LANE_GROUPS = LANES // S5_GROUP_SIZE


def _dot(a, b):
    return jnp.dot(a, b, preferred_element_type=F32)


def _dot_nt(a, b):
    return lax.dot_general(a, b, (((1,), (1,)), ((), ())), preferred_element_type=F32)


def _dot_tn(a, b):
    return lax.dot_general(a, b, (((0,), (0,)), ((), ())), preferred_element_type=F32)


def _rms(x, g):
    ms = jnp.mean(x * x, axis=-1, keepdims=True)
    return x * lax.rsqrt(ms + NORM_EPS) * g


def _log_sigmoid(x):
    return jnp.minimum(x, 0.0) - jnp.log1p(jnp.exp(-jnp.abs(x)))


def _split_dot(mask_bf16, x):
    hi = x.astype(BF16)
    lo = (x - hi.astype(F32)).astype(BF16)
    return _dot(mask_bf16, hi) + _dot(mask_bf16, lo)


def _full_spec(arr):
    nd = arr.ndim
    return pl.BlockSpec(arr.shape, lambda b, l, _nd=nd: (0,) * _nd,
                        pipeline_mode=pl.Buffered(1))


def _tile_spec(tile, d):
    return pl.BlockSpec((None, tile, d), lambda b, l: (b, l, 0))


def _compiler_params():
    return pltpu.CompilerParams(dimension_semantics=("arbitrary", "arbitrary"),
                                vmem_limit_bytes=VMEM_LIMIT_BYTES)


def _ffn_kernel(x_ref, lnpre_ref, lnpost_ref, wup_ref, cw_ref, cb_ref, wd_ref,
                o_ref, h_ref, act_ref, tail_ref):
    tile = x_ref.shape[0]
    f = wd_ref.shape[0]
    n_chunks = tail_ref.shape[0]
    fc = tail_ref.shape[2]

    @pl.when(pl.program_id(1) == 0)
    def _():
        tail_ref[...] = jnp.zeros_like(tail_ref)

    x = x_ref[...]
    h_ref[...] = _rms(x, lnpre_ref[...]).astype(BF16)

    def conv3(g, w):
        return (w[0:1] * pltpu.roll(g, 2, axis=0) + w[1:2] * pltpu.roll(g, 1, axis=0)
                + w[2:3] * g)

    def up(c):
        h = h_ref[...]
        return (_dot(h, wup_ref[:, c * fc:(c + 1) * fc]),
                _dot(h, wup_ref[:, f + c * fc:f + (c + 1) * fc]))

    def gate(c, g, u):
        w = cw_ref[:, c * fc:(c + 1) * fc]
        b = cb_ref[:, c * fc:(c + 1) * fc]
        conv = conv3(g, w) + b
        head = jnp.concatenate([tail_ref[c], g[0:SUBLANES]], axis=0)
        conv_head = (conv3(head, w) + b)[SUBLANES:2 * SUBLANES]
        tail_ref[c] = g[tile - SUBLANES:tile]
        conv = jnp.concatenate([conv_head, conv[SUBLANES:]], axis=0)
        act_ref[:, c * fc:(c + 1) * fc] = (jax.nn.gelu(conv, approximate=True) * u).astype(BF16)

    gu = up(0)
    for c in range(n_chunks):
        gu_next = up(c + 1) if c + 1 < n_chunks else None
        gate(c, *gu)
        gu = gu_next
    o_ref[...] = x + _rms(_dot(act_ref[...], wd_ref[...]), lnpost_ref[...])


def _ffn_layer(x, ln_pre, ln_post, w_up, conv_w, conv_b, w_down):
    bsz, seq, d = x.shape
    f = w_down.shape[0]
    tile = min(FFN_TILE, seq)
    nc = f // FFN_COLS
    taps = conv_w.shape[0]
    cw = jnp.zeros((SUBLANES, f), F32).at[:taps].set(conv_w)
    args = (ln_pre.reshape(1, d), ln_post.reshape(1, d), w_up.astype(BF16), cw, conv_b.reshape(1, f),
            w_down.astype(BF16))
    return pl.pallas_call(
        _ffn_kernel,
        grid=(bsz, seq // tile),
        in_specs=[_tile_spec(tile, d)] + [_full_spec(a) for a in args],
        out_specs=_tile_spec(tile, d),
        out_shape=jax.ShapeDtypeStruct(x.shape, x.dtype),
        scratch_shapes=[pltpu.VMEM((tile, d), BF16), pltpu.VMEM((tile, f), BF16),
                        pltpu.VMEM((nc, SUBLANES, FFN_COLS), F32)],
        compiler_params=_compiler_params(),
        name="conv_ffn",
    )(x, *args)


def _even_kernel(x_ref, lnpre_ref, lnpost_ref, wmain_ref, wa_ref, wup_ref, balpha_ref, gnorm_ref,
                 lam_ref, lamc_ref, bmat_ref, cmat_ref, dskip_ref, wglu_ref, bglu_ref, wout_ref,
                 causalf_ref, causalb_ref, inchunk_ref, unperm_ref,
                 o_ref,
                 h_ref, unat_ref, uperm_ref, bu_ref, xs_ref, yperm_ref, e_ref, cin_ref, anat_ref,
                 s5carry_ref, ogla_ref, glastate_ref):
    tile = x_ref.shape[0]
    n_blk = unat_ref.shape[0]
    sw = n_blk * LANES
    half = bu_ref.shape[2] // 2
    steps = S5_STEPS
    rows = tile // steps
    kw = wup_ref.shape[1]
    vw = ogla_ref.shape[1]
    dk = kw // GLA_HEADS
    dv = vw // GLA_HEADS
    n_chunks = tile // GLA_CHUNK

    @pl.when(pl.program_id(1) == 0)
    def _():
        s5carry_ref[...] = jnp.zeros_like(s5carry_ref)
        glastate_ref[...] = jnp.zeros_like(glastate_ref)

    x = x_ref[...]
    h_ref[...] = _rms(x, lnpre_ref[...]).astype(BF16)
    h = h_ref[...]

    u = _dot(h, wmain_ref[:, 0:sw])
    for blk in range(n_blk):
        unat_ref[blk] = u[:, blk * LANES:(blk + 1) * LANES]
    for blk in range(n_blk):
        for s in range(steps):
            uperm_ref[s * rows:(s + 1) * rows, blk * LANES:(blk + 1) * LANES] = (
                unat_ref[blk, pl.ds(s, rows, stride=steps), :])

    def b_proj(blk):
        bu_ref[blk % 2] = _dot(uperm_ref[:, blk * LANES:(blk + 1) * LANES].astype(BF16),
                               bmat_ref[blk])

    def s5_block(blk):
        lanes = slice(blk * LANES, (blk + 1) * LANES)
        slot = blk % 2
        a_re = lam_ref[blk, 0:1, :]
        a_im = lam_ref[blk, 1:2, :]

        def advance(s, state, store):
            s_re, s_im = state
            b = bu_ref[slot, s * rows:(s + 1) * rows, :]
            n_re = a_re * s_re - a_im * s_im + b[:, :half]
            n_im = a_re * s_im + a_im * s_re + b[:, half:]
            if store:
                xs_ref[slot, s * rows:(s + 1) * rows, :] = jnp.concatenate([n_re, n_im], axis=1)
            return n_re, n_im

        state = (jnp.zeros((rows, half), F32), jnp.zeros((rows, half), F32))
        for s in range(steps):
            state = advance(s, state, False)
        e_ref[...] = jnp.concatenate(state, axis=1)
        ac_re = lamc_ref[blk, 0:1, :]
        ac_im = lamc_ref[blk, 1:2, :]
        c_re = s5carry_ref[blk, 0:1, :]
        c_im = s5carry_ref[blk, 1:2, :]
        for i in range(rows):
            cin_ref[i:i + 1, :] = jnp.concatenate([c_re, c_im], axis=1)
            e_row = e_ref[i:i + 1, :]
            c_re, c_im = (ac_re * c_re - ac_im * c_im + e_row[:, :half],
                          ac_re * c_im + ac_im * c_re + e_row[:, half:])
        s5carry_ref[blk, 0:1, :] = c_re
        s5carry_ref[blk, 1:2, :] = c_im
        cin = cin_ref[...]
        state = (cin[:, :half], cin[:, half:])
        for s in range(steps):
            state = advance(s, state, True)
        yperm_ref[:, lanes] = _dot(xs_ref[slot].astype(BF16), cmat_ref[blk])

    for blk in range(min(2, n_blk)):
        b_proj(blk)

    o0 = sw
    r0 = o0 + 2 * kw + vw
    q = _dot(h, wmain_ref[:, o0:o0 + kw]) * (dk ** -0.5)
    k = _dot(h, wmain_ref[:, o0 + kw:o0 + 2 * kw])
    v = _dot(h, wmain_ref[:, o0 + 2 * kw:o0 + 2 * kw + vw]).astype(BF16)
    a_lr = _dot(h, wa_ref[...])
    z = _dot(a_lr.astype(BF16), wup_ref[...]) + balpha_ref[...]
    log_a = _log_sigmoid(z) * (1.0 / GLA_GATE_TEMP)

    g = _split_dot(causalb_ref[...], log_a)
    g_last = _split_dot(inchunk_ref[...], log_a)
    q_dec = (q * jnp.exp(g)).astype(BF16)
    k_dec = (k * jnp.exp(-g)).astype(BF16)
    k_upd = (k * jnp.exp(g_last - g)).astype(BF16)
    decay = jnp.exp(g_last)
    keep = causalf_ref[...] > 0.5
    gate = jax.nn.silu(_dot(h, wmain_ref[:, r0:r0 + vw]))

    def gla_head(hd):
        ks = slice(hd * dk, (hd + 1) * dk)
        vs = slice(hd * dv, (hd + 1) * dv)
        qh, kh, kuh, vh = q_dec[:, ks], k_dec[:, ks], k_upd[:, ks], v[:, vs]
        scores = jnp.where(keep, _dot_nt(qh, kh), 0.0).astype(BF16)
        o_intra = _dot(scores, vh)
        chunks = [slice(c * GLA_CHUNK, (c + 1) * GLA_CHUNK) for c in range(n_chunks)]
        d_st = [_dot_tn(vh[rs], kuh[rs]) for rs in chunks]
        st = glastate_ref[hd]
        states = []
        for c in range(n_chunks):
            states.append(st.astype(BF16))
            st = st * decay[c * GLA_CHUNK:c * GLA_CHUNK + 1, ks] + d_st[c]
        glastate_ref[hd] = st
        for c, rs in enumerate(chunks):
            ogla_ref[rs, vs] = o_intra[rs] + _dot_nt(qh[rs], states[c])
        oh = ogla_ref[:, vs]
        ms = jnp.mean(oh * oh, axis=-1, keepdims=True)
        anat_ref[:, sw + hd * dv:sw + (hd + 1) * dv] = (
            oh * lax.rsqrt(ms + NORM_EPS) * gnorm_ref[:, vs] * gate[:, vs]).astype(BF16)

    s5_block(0)
    for i in range(max(n_blk - 1, GLA_HEADS)):
        if i + 2 < n_blk:
            b_proj(i + 2)
        if i < GLA_HEADS:
            gla_head(i)
        if i + 1 < n_blk:
            s5_block(i + 1)

    y = yperm_ref[...] + dskip_ref[...] * uperm_ref[...]
    y = jax.nn.gelu(y, approximate=True)
    a_perm = y * jax.nn.sigmoid(_dot(y.astype(BF16), wglu_ref[...]) + bglu_ref[...])
    anat_ref[:, 0:sw] = _dot(unperm_ref[...], a_perm.astype(BF16)).astype(BF16)

    o_ref[...] = x + _rms(_dot(anat_ref[...], wout_ref[...]), lnpost_ref[...])


def _s5_discretize(lam_re, lam_im, b_re, b_im, log_dt, power):
    dt = jnp.exp(log_dt)[:, None]
    mag = jnp.exp(lam_re * dt)
    lb_re = mag * jnp.cos(lam_im * dt)
    lb_im = mag * jnp.sin(lam_im * dt)
    inv = 1.0 / (lam_re * lam_re + lam_im * lam_im)
    zr = ((lb_re - 1.0) * lam_re + lb_im * lam_im) * inv
    zi = (lb_im * lam_re - (lb_re - 1.0) * lam_im) * inv
    bb_re = zr[..., None] * b_re - zi[..., None] * b_im
    bb_im = zr[..., None] * b_im + zi[..., None] * b_re
    magp = jnp.exp(lam_re * dt * power)
    lp_re = magp * jnp.cos(lam_im * dt * power)
    lp_im = magp * jnp.sin(lam_im * dt * power)
    return lb_re, lb_im, lp_re, lp_im, bb_re, bb_im


def _even_layer(x, ln_pre, ln_post, w_in, lam_re, lam_im, b_re, b_im, c_re, c_im, d_skip, log_dt,
                w_glu, b_glu, w_alpha_up, b_alpha, gla_norm, w_out):
    bsz, seq, d = x.shape
    groups, state = lam_re.shape
    gs = b_re.shape[-1]
    sw = groups * gs
    rank, kw = w_alpha_up.shape
    vw = gla_norm.shape[0]
    tile = min(EVEN_TILE, seq)
    lg = S5_LANE_GROUPS
    n_blk = groups // lg
    half = lg * state

    lb_re, lb_im, lp_re, lp_im, bb_re, bb_im = _s5_discretize(
        lam_re.astype(F32), lam_im.astype(F32), b_re.astype(F32), b_im.astype(F32),
        log_dt.astype(F32), float(S5_STEPS))

    def lam_pack(re, im):
        out = jnp.zeros((n_blk, SUBLANES, half), F32)
        return out.at[:, 0].set(re.reshape(n_blk, half)).at[:, 1].set(im.reshape(n_blk, half))

    eye = jnp.eye(lg, dtype=F32)

    def b_pack(bb):
        t = bb.reshape(n_blk, lg, state, gs)
        return jnp.einsum('bgpj,gh->bgjhp', t, eye).reshape(n_blk, lg * gs, half)

    def c_pack(cc):
        t = cc.reshape(n_blk, lg, gs, state)
        return jnp.einsum('bgip,gh->bhpgi', t, eye).reshape(n_blk, half, lg * gs)

    bmat = jnp.concatenate([b_pack(bb_re), b_pack(bb_im)], axis=2).astype(BF16)
    cmat = jnp.concatenate([c_pack(c_re.astype(F32)), -c_pack(c_im.astype(F32))], axis=1).astype(BF16)
    main = sw + 2 * kw + 2 * vw
    wa = jnp.zeros((d, LANES), F32).at[:, :rank].set(w_in[:, main:main + rank]).astype(BF16)
    wup = jnp.zeros((LANES, kw), F32).at[:rank].set(w_alpha_up).astype(BF16)
    args = (ln_pre.reshape(1, d), ln_post.reshape(1, d), w_in[:, :main].astype(BF16), wa, wup,
            b_alpha.reshape(1, kw), gla_norm.reshape(1, vw),
            lam_pack(lb_re, lb_im), lam_pack(lp_re, lp_im), bmat, cmat,
            d_skip.reshape(1, sw), w_glu.astype(BF16), b_glu.reshape(1, sw), w_out.astype(BF16))
    rows = tile // S5_STEPS
    r_idx = jnp.arange(tile, dtype=jnp.int32)[:, None]
    c_idx = jnp.arange(tile, dtype=jnp.int32)[None, :]
    in_chunk = (r_idx // GLA_CHUNK == c_idx // GLA_CHUNK)
    causal = (in_chunk & (c_idx <= r_idx)).astype(F32)
    unperm = (c_idx == (r_idx % S5_STEPS) * rows + r_idx // S5_STEPS)
    args = args + (causal, causal.astype(BF16), in_chunk.astype(BF16), unperm.astype(BF16))
    scratch = [
        pltpu.VMEM((tile, d), BF16),
        pltpu.VMEM((n_blk, tile, LANES), F32),
        pltpu.VMEM((tile, sw), F32),
        pltpu.VMEM((2, tile, 2 * half), F32),
        pltpu.VMEM((2, tile, 2 * half), F32),
        pltpu.VMEM((tile, sw), F32),
        pltpu.VMEM((rows, 2 * half), F32),
        pltpu.VMEM((rows, 2 * half), F32),
        pltpu.VMEM((tile, sw + vw), BF16),
        pltpu.VMEM((n_blk, SUBLANES, half), F32),
        pltpu.VMEM((tile, vw), F32),
        pltpu.VMEM((GLA_HEADS, vw // GLA_HEADS, kw // GLA_HEADS), F32),
    ]
    return pl.pallas_call(
        _even_kernel,
        grid=(bsz, seq // tile),
        in_specs=[_tile_spec(tile, d)] + [_full_spec(a) for a in args],
        out_specs=_tile_spec(tile, d),
        out_shape=jax.ShapeDtypeStruct(x.shape, x.dtype),
        scratch_shapes=scratch,
        compiler_params=_compiler_params(),
        name="even_mixer",
    )(x, *args)


def kernel(x, ln_mix_pre, ln_mix_post, ln_ffn_pre, ln_ffn_post, ev_w_in, s5_lambda_re, s5_lambda_im, s5_b_re, s5_b_im, s5_c_re, s5_c_im, s5_d, s5_log_dt, s5_w_glu, s5_b_glu, gla_w_alpha_up, gla_b_alpha, gla_norm, ev_w_out, od_w_in, ml_conv_w, ml_conv_b, ml_w_q, ml_w_k, ml_w_v, ml_w_gate, ml_b_gate, ml_norm, ml_skip, od_w_out, ffn_w_up, ffn_conv_w, ffn_conv_b, ffn_w_down):
    depth = ln_mix_pre.shape[0]
    for layer in range(depth):
        if layer % 2 == 0:
            e = layer // 2
            x = _even_layer(x, ln_mix_pre[layer], ln_mix_post[layer], ev_w_in[e], s5_lambda_re[e],
                            s5_lambda_im[e], s5_b_re[e], s5_b_im[e], s5_c_re[e], s5_c_im[e], s5_d[e],
                            s5_log_dt[e], s5_w_glu[e], s5_b_glu[e], gla_w_alpha_up[e], gla_b_alpha[e],
                            gla_norm[e], ev_w_out[e])
        else:
            o = layer // 2
            x = _odd_layer(x, ln_mix_pre[layer], ln_mix_post[layer], od_w_in[o], ml_conv_w[o],
                           ml_conv_b[o], ml_w_q[o], ml_w_k[o], ml_w_v[o], ml_w_gate[o], ml_b_gate[o],
                           ml_norm[o], ml_skip[o], od_w_out[o])
        x = _ffn_layer(x, ln_ffn_pre[layer], ln_ffn_post[layer], ffn_w_up[layer], ffn_conv_w[layer],
                       ffn_conv_b[layer], ffn_w_down[layer])
    return x


def _odd_kernel(x_ref, lnpre_ref, lnpost_ref, win_ref, cw_ref, cb_ref, wqk_ref, wv_ref, wg_ref, bg_ref,
                mnorm_ref, skip_ref, wout_ref,
                o_ref,
                h_ref, xc_ref, xcb_ref, xmb_ref, q_ref, k_ref, v_ref, tail_ref, c_ref, n_ref,
                m_ref):
    tile = x_ref.shape[0]
    inner = xc_ref.shape[1]
    heads = c_ref.shape[0]
    dh = inner // heads
    n_blocks = wqk_ref.shape[0]
    bw = wqk_ref.shape[1]
    taps = 4

    @pl.when(pl.program_id(1) == 0)
    def _():
        tail_ref[...] = jnp.zeros_like(tail_ref)
        c_ref[...] = jnp.zeros_like(c_ref)
        n_ref[...] = jnp.zeros_like(n_ref)
        m_ref[...] = jnp.zeros_like(m_ref)

    x = x_ref[...]
    h_ref[...] = _rms(x, lnpre_ref[...]).astype(BF16)

    def conv(xm, w):
        out = w[taps - 1:taps] * xm
        for kk in range(taps - 1):
            out = out + w[kk:kk + 1] * pltpu.roll(xm, taps - 1 - kk, axis=0)
        return out

    gw = 2 * bw
    n_groups = inner // gw

    def in_proj(grp):
        return _dot(h_ref[...], win_ref[:, grp * gw:(grp + 1) * gw])

    def conv_silu(grp, xm):
        ls = slice(grp * gw, (grp + 1) * gw)
        w = cw_ref[:, ls]
        b = cb_ref[:, ls]
        pre = conv(xm, w) + b
        head = jnp.concatenate([tail_ref[:, ls], xm[0:SUBLANES]], axis=0)
        pre_head = (conv(head, w) + b)[SUBLANES:2 * SUBLANES]
        tail_ref[:, ls] = xm[tile - SUBLANES:tile]
        xc = jax.nn.silu(jnp.concatenate([pre_head, pre[SUBLANES:]], axis=0))
        xc_ref[:, ls] = xc
        xcb_ref[:, ls] = xc.astype(BF16)
        xmb_ref[:, ls] = xm.astype(BF16)

    def headwise(grp):
        for blk in range(grp * (gw // bw), (grp + 1) * (gw // bw)):
            ls = slice(blk * bw, (blk + 1) * bw)
            qk = _dot(xcb_ref[:, ls], wqk_ref[blk])
            q_ref[:, ls] = qk[:, :bw].astype(BF16)
            k_ref[:, ls] = qk[:, bw:].astype(BF16)
            v_ref[:, ls] = _dot(xmb_ref[:, ls], wv_ref[blk]).astype(BF16)

    xm_next = in_proj(0)
    for grp in range(n_groups):
        xm = xm_next
        if grp + 1 < n_groups:
            xm_next = in_proj(grp + 1)
        if grp > 0:
            headwise(grp - 1)
        conv_silu(grp, xm)
    headwise(n_groups - 1)

    gates = (_dot(q_ref[...], wg_ref[0]) + _dot(k_ref[...], wg_ref[1]) + _dot(v_ref[...], wg_ref[2])
             + bg_ref[...])
    log_f = _log_sigmoid(gates)
    row = lax.broadcasted_iota(jnp.int32, (tile, tile), 0)
    col = lax.broadcasted_iota(jnp.int32, (tile, tile), 1)
    causal = col <= row
    f_cum = _split_dot(jnp.where(causal, 1.0, 0.0).astype(BF16), log_f)
    gates_t = jnp.transpose(gates)
    f_cum_t = jnp.transpose(f_cum)
    scale = dh ** -0.5

    def state_matmuls(hd):
        hs = slice(hd * dh, (hd + 1) * dh)
        qh = q_ref[:, hs]
        c_mat = c_ref[hd]
        return _dot_nt(qh, k_ref[:, hs]), _dot(qh, c_mat.astype(BF16)), c_mat

    ahead = state_matmuls(0)
    for hd in range(heads):
        qk_raw, inter_raw, c_mat = ahead
        if hd + 1 < heads:
            ahead = state_matmuls(hd + 1)
        hs = slice(hd * dh, (hd + 1) * dh)
        f_col = f_cum[:, heads + hd:heads + hd + 1]
        f_row = f_cum_t[heads + hd:heads + hd + 1, :]
        i_col = gates[:, hd:hd + 1]
        i_row = gates_t[hd:hd + 1, :]
        m_prev = m_ref[hd:hd + 1, 0:1]
        d_log = jnp.where(causal, f_col - f_row + i_row, -jnp.inf)
        inter_log = f_col + m_prev
        m_loc = jnp.maximum(inter_log, jnp.max(d_log, axis=-1, keepdims=True))
        qh = q_ref[:, hs]
        kh = k_ref[:, hs]
        vh = v_ref[:, hs]
        s = qk_raw * (scale * jnp.exp(d_log - m_loc))
        w_inter = jnp.exp(inter_log - m_loc)
        num = _dot(s.astype(BF16), vh) + w_inter * inter_raw
        o_pre = _dot(h_ref[...], win_ref[:, inner + hd * dh:inner + (hd + 1) * dh])
        n_vec = n_ref[hd:hd + 1, :]
        den = (jnp.sum(s, axis=-1, keepdims=True)
               + w_inter * jnp.sum(qh.astype(F32) * n_vec, axis=-1, keepdims=True))
        denom = jnp.maximum(jnp.abs(den), jnp.exp(-m_loc))
        hh = num * (1.0 / denom)
        ms = jnp.mean(hh * hh, axis=-1, keepdims=True)
        hn = hh * lax.rsqrt(ms + NORM_EPS) * mnorm_ref[:, hs] + skip_ref[:, hs] * xc_ref[:, hs]
        gated = (jax.nn.sigmoid(o_pre) * hn).astype(BF16)
        part = _dot(gated, wout_ref[hs, :])
        mix = part if hd == 0 else mix + part
        f_last = f_col[tile - 1:tile, :]
        w_log = f_last - f_col + i_col
        m_new = jnp.maximum(f_last + m_prev, jnp.max(w_log, axis=0, keepdims=True))
        w_upd = jnp.exp(w_log - m_new)
        decay = jnp.exp(f_last + m_prev - m_new)
        kw = kh.astype(F32) * (scale * w_upd)
        c_ref[hd] = decay * c_mat + _dot_tn(kw.astype(BF16), vh)
        n_ref[hd:hd + 1, :] = decay * n_vec + jnp.sum(kw, axis=0, keepdims=True)
        m_ref[hd:hd + 1, :] = jnp.broadcast_to(m_new, (1, m_ref.shape[1]))

    o_ref[...] = x + _rms(mix, lnpost_ref[...])


def _odd_layer(x, ln_pre, ln_post, w_in, conv_w, conv_b, w_q, w_k, w_v, w_gate, b_gate, m_norm, skip,
               w_out):
    bsz, seq, d = x.shape
    inner = w_out.shape[0]
    heads = b_gate.shape[0] // 2
    tile = min(ODD_TILE, seq)
    bw = MXU_DIM
    n_blocks = inner // bw
    per = bw // MLSTM_QKV_BLOCK
    eye = jnp.eye(per, dtype=F32)

    def blockdiag(w):
        t = w.astype(F32).reshape(n_blocks, per, MLSTM_QKV_BLOCK, MLSTM_QKV_BLOCK)
        return jnp.einsum('bncd,nm->bncmd', t, eye).reshape(n_blocks, bw, bw)

    wqk = jnp.concatenate([blockdiag(w_q), blockdiag(w_k)], axis=2).astype(BF16)
    wv = blockdiag(w_v).astype(BF16)
    wg = jnp.zeros((3, inner, LANES), F32).at[:, :, :2 * heads].set(w_gate).astype(BF16)
    bg = jnp.zeros((1, LANES), F32).at[0, :2 * heads].set(b_gate)
    cw = jnp.zeros((SUBLANES, inner), F32).at[:conv_w.shape[0]].set(conv_w)
    args = (ln_pre.reshape(1, d), ln_post.reshape(1, d), w_in.astype(BF16), cw, conv_b.reshape(1, inner),
            wqk, wv, wg, bg, m_norm.reshape(1, inner), skip.reshape(1, inner), w_out.astype(BF16))
    dh = inner // heads
    scratch = [
        pltpu.VMEM((tile, d), BF16),
        pltpu.VMEM((tile, inner), F32),
        pltpu.VMEM((tile, inner), BF16),
        pltpu.VMEM((tile, inner), BF16),
        pltpu.VMEM((tile, inner), BF16),
        pltpu.VMEM((tile, inner), BF16),
        pltpu.VMEM((tile, inner), BF16),
        pltpu.VMEM((SUBLANES, inner), F32),
        pltpu.VMEM((heads, dh, dh), F32),
        pltpu.VMEM((SUBLANES, dh), F32),
        pltpu.VMEM((SUBLANES, LANES), F32),
    ]
    return pl.pallas_call(
        _odd_kernel,
        grid=(bsz, seq // tile),
        in_specs=[_tile_spec(tile, d)] + [_full_spec(a) for a in args],
        out_specs=_tile_spec(tile, d),
        out_shape=jax.ShapeDtypeStruct(x.shape, x.dtype),
        scratch_shapes=scratch,
        compiler_params=_compiler_params(),
        name="odd_mixer",
    )(x, *args)
```

```python
import functools
import math

import jax
import jax.numpy as jnp
from jax import lax
from jax.experimental import pallas as pl
from jax.experimental.pallas import tpu as pltpu

F32 = jnp.float32
BF16 = jnp.bfloat16

NORM_EPS = 1e-6
DT_UNUSED = None

S5_GROUP_SIZE = 16
S5_STATE = 64
GLA_HEADS = 4
GLA_CHUNK = 64
GLA_GATE_TEMP = 16.0
MLSTM_HEADS = 4
MLSTM_QKV_BLOCK = 4

LANES = 128
SUBLANES = 8
MXU_DIM = 256
VMEM_LIMIT_BYTES = 58 * 1024 * 1024

EVEN_TILE = 512
S5_STEPS = 32
ODD_TILE = 256
FFN_TILE = 512
FFN_COLS = 256
S5_LANE_GROUPS = LANES // S5_GROUP_SIZE
GLA_SUB = MXU_DIM


def _dot(a, b):
    return jnp.dot(a, b, preferred_element_type=F32)


def _dot_nt(a, b):
    return lax.dot_general(a, b, (((1,), (1,)), ((), ())), preferred_element_type=F32)


def _dot_tn(a, b):
    return lax.dot_general(a, b, (((0,), (0,)), ((), ())), preferred_element_type=F32)


def _rms(x, g):
    ms = jnp.mean(x * x, axis=-1, keepdims=True)
    return x * lax.rsqrt(ms + NORM_EPS) * g


def _log_sigmoid(x):
    return jnp.minimum(x, 0.0) - jnp.log1p(jnp.exp(-jnp.abs(x)))


def _split_dot(mask_bf16, x):
    hi = x.astype(BF16)
    lo = (x - hi.astype(F32)).astype(BF16)
    return _dot(mask_bf16, hi) + _dot(mask_bf16, lo)


def _full_spec(arr):
    nd = arr.ndim
    return pl.BlockSpec(arr.shape, lambda b, l, _nd=nd: (0,) * _nd,
                        pipeline_mode=pl.Buffered(1))


def _tile_spec(tile, d):
    return pl.BlockSpec((None, tile, d), lambda b, l: (b, l, 0))


def _compiler_params():
    return pltpu.CompilerParams(dimension_semantics=("arbitrary", "arbitrary"),
                                vmem_limit_bytes=VMEM_LIMIT_BYTES)


def _ffn_kernel(x_ref, lnpre_ref, lnpost_ref, wup_ref, cw_ref, cb_ref, wd_ref,
                o_ref, h_ref, act_ref, tail_ref):
    tile = x_ref.shape[0]
    f = wd_ref.shape[0]
    n_chunks = tail_ref.shape[0]
    fc = tail_ref.shape[2]

    @pl.when(pl.program_id(1) == 0)
    def _():
        tail_ref[...] = jnp.zeros_like(tail_ref)

    x = x_ref[...]
    h_ref[...] = _rms(x, lnpre_ref[...]).astype(BF16)

    def conv3(g, w):
        return (w[0:1] * pltpu.roll(g, 2, axis=0) + w[1:2] * pltpu.roll(g, 1, axis=0)
                + w[2:3] * g)

    def up(c):
        h = h_ref[...]
        return (_dot(h, wup_ref[:, c * fc:(c + 1) * fc]),
                _dot(h, wup_ref[:, f + c * fc:f + (c + 1) * fc]))

    def gate(c, g, u):
        w = cw_ref[:, c * fc:(c + 1) * fc]
        b = cb_ref[:, c * fc:(c + 1) * fc]
        conv = conv3(g, w) + b
        head = jnp.concatenate([tail_ref[c], g[0:SUBLANES]], axis=0)
        conv_head = (conv3(head, w) + b)[SUBLANES:2 * SUBLANES]
        tail_ref[c] = g[tile - SUBLANES:tile]
        conv = jnp.concatenate([conv_head, conv[SUBLANES:]], axis=0)
        act_ref[:, c * fc:(c + 1) * fc] = (jax.nn.gelu(conv, approximate=True) * u).astype(BF16)

    gu = up(0)
    for c in range(n_chunks):
        gu_next = up(c + 1) if c + 1 < n_chunks else None
        gate(c, *gu)
        gu = gu_next
    o_ref[...] = x + _rms(_dot(act_ref[...], wd_ref[...]), lnpost_ref[...])


def _ffn_layer(x, ln_pre, ln_post, w_up, conv_w, conv_b, w_down):
    bsz, seq, d = x.shape
    f = w_down.shape[0]
    tile = min(FFN_TILE, seq)
    nc = f // FFN_COLS
    taps = conv_w.shape[0]
    cw = jnp.zeros((SUBLANES, f), F32).at[:taps].set(conv_w)
    args = (ln_pre.reshape(1, d), ln_post.reshape(1, d), w_up.astype(BF16), cw, conv_b.reshape(1, f),
            w_down.astype(BF16))
    return pl.pallas_call(
        _ffn_kernel,
        grid=(bsz, seq // tile),
        in_specs=[_tile_spec(tile, d)] + [_full_spec(a) for a in args],
        out_specs=_tile_spec(tile, d),
        out_shape=jax.ShapeDtypeStruct(x.shape, x.dtype),
        scratch_shapes=[pltpu.VMEM((tile, d), BF16), pltpu.VMEM((tile, f), BF16),
                        pltpu.VMEM((nc, SUBLANES, FFN_COLS), F32)],
        compiler_params=_compiler_params(),
        name="conv_ffn",
    )(x, *args)


def _even_kernel(x_ref, lnpre_ref, lnpost_ref, wmain_ref, wa_ref, wup_ref, balpha_ref, gnorm_ref,
                 lam_ref, lamc_ref, bmat_ref, cmat_ref, dskip_ref, wglu_ref, bglu_ref, wout_ref,
                 causalf_ref, causalb_ref, inchunk_ref, unperm_ref,
                 o_ref,
                 h_ref, unat_ref, uperm_ref, bu_ref, xs_ref, yperm_ref, e_ref, cin_ref, anat_ref,
                 s5carry_ref, ogla_ref, glastate_ref):
    tile = x_ref.shape[0]
    n_blk = unat_ref.shape[0]
    sw = n_blk * LANES
    half = bu_ref.shape[2] // 2
    steps = S5_STEPS
    rows = tile // steps
    kw = wup_ref.shape[1]
    vw = ogla_ref.shape[1]
    dk = kw // GLA_HEADS
    dv = vw // GLA_HEADS
    n_chunks = tile // GLA_CHUNK

    @pl.when(pl.program_id(1) == 0)
    def _():
        s5carry_ref[...] = jnp.zeros_like(s5carry_ref)
        glastate_ref[...] = jnp.zeros_like(glastate_ref)

    x = x_ref[...]
    h_ref[...] = _rms(x, lnpre_ref[...]).astype(BF16)
    h = h_ref[...]

    u = _dot(h, wmain_ref[:, 0:sw])
    for blk in range(n_blk):
        unat_ref[blk] = u[:, blk * LANES:(blk + 1) * LANES]
    for blk in range(n_blk):
        for s in range(steps):
            uperm_ref[s * rows:(s + 1) * rows, blk * LANES:(blk + 1) * LANES] = (
                unat_ref[blk, pl.ds(s, rows, stride=steps), :])

    def b_proj(blk):
        bu_ref[blk % 2] = _dot(uperm_ref[:, blk * LANES:(blk + 1) * LANES].astype(BF16),
                               bmat_ref[blk])

    def s5_block(blk):
        lanes = slice(blk * LANES, (blk + 1) * LANES)
        slot = blk % 2
        a_re = lam_ref[blk, 0:1, :]
        a_im = lam_ref[blk, 1:2, :]

        def advance(s, state, store):
            s_re, s_im = state
            b = bu_ref[slot, s * rows:(s + 1) * rows, :]
            n_re = a_re * s_re - a_im * s_im + b[:, :half]
            n_im = a_re * s_im + a_im * s_re + b[:, half:]
            if store:
                xs_ref[slot, s * rows:(s + 1) * rows, :] = jnp.concatenate([n_re, n_im], axis=1)
            return n_re, n_im

        state = (jnp.zeros((rows, half), F32), jnp.zeros((rows, half), F32))
        for s in range(steps):
            state = advance(s, state, False)
        e_ref[...] = jnp.concatenate(state, axis=1)
        ac_re = lamc_ref[blk, 0:1, :]
        ac_im = lamc_ref[blk, 1:2, :]
        c_re = s5carry_ref[blk, 0:1, :]
        c_im = s5carry_ref[blk, 1:2, :]
        for i in range(rows):
            cin_ref[i:i + 1, :] = jnp.concatenate([c_re, c_im], axis=1)
            e_row = e_ref[i:i + 1, :]
            c_re, c_im = (ac_re * c_re - ac_im * c_im + e_row[:, :half],
                          ac_re * c_im + ac_im * c_re + e_row[:, half:])
        s5carry_ref[blk, 0:1, :] = c_re
        s5carry_ref[blk, 1:2, :] = c_im
        cin = cin_ref[...]
        state = (cin[:, :half], cin[:, half:])
        for s in range(steps):
            state = advance(s, state, True)
        yperm_ref[:, lanes] = _dot(xs_ref[slot].astype(BF16), cmat_ref[blk])

    for blk in range(min(2, n_blk)):
        b_proj(blk)

    o0 = sw
    r0 = o0 + 2 * kw + vw
    qk = _dot(h, wmain_ref[:, o0:o0 + 2 * kw])
    q = qk[:, :kw] * (dk ** -0.5)
    k = qk[:, kw:]
    v = _dot(h, wmain_ref[:, o0 + 2 * kw:o0 + 2 * kw + vw]).astype(BF16)
    a_lr = _dot(h, wa_ref[...])
    z = _dot(a_lr.astype(BF16), wup_ref[...]) + balpha_ref[...]
    log_a = _log_sigmoid(z) * (1.0 / GLA_GATE_TEMP)

    sub = causalf_ref.shape[0]
    subs = [slice(i * sub, (i + 1) * sub) for i in range(tile // sub)]
    g = jnp.concatenate([_split_dot(causalb_ref[...], log_a[rs]) for rs in subs], axis=0)
    g_last = jnp.concatenate([_split_dot(inchunk_ref[...], log_a[rs]) for rs in subs], axis=0)
    q_dec = (q * jnp.exp(g)).astype(BF16)
    k_dec = (k * jnp.exp(-g)).astype(BF16)
    k_upd = (k * jnp.exp(g_last - g)).astype(BF16)
    decay = jnp.exp(g_last)
    keep = causalf_ref[...] > 0.5
    gate = jax.nn.silu(_dot(h, wmain_ref[:, r0:r0 + vw]))

    def gla_head(hd):
        ks = slice(hd * dk, (hd + 1) * dk)
        vs = slice(hd * dv, (hd + 1) * dv)
        qh, kh, kuh, vh = q_dec[:, ks], k_dec[:, ks], k_upd[:, ks], v[:, vs]
        o_intra = jnp.concatenate(
            [_dot(jnp.where(keep, _dot_nt(qh[rs], kh[rs]), 0.0).astype(BF16), vh[rs]) for rs in subs],
            axis=0)
        chunks = [slice(c * GLA_CHUNK, (c + 1) * GLA_CHUNK) for c in range(n_chunks)]
        d_st = [_dot_tn(vh[rs], kuh[rs]) for rs in chunks]
        st = glastate_ref[hd]
        states = []
        for c in range(n_chunks):
            states.append(st.astype(BF16))
            st = st * decay[c * GLA_CHUNK:c * GLA_CHUNK + 1, ks] + d_st[c]
        glastate_ref[hd] = st
        for c, rs in enumerate(chunks):
            ogla_ref[rs, vs] = o_intra[rs] + _dot_nt(qh[rs], states[c])
        oh = ogla_ref[:, vs]
        ms = jnp.mean(oh * oh, axis=-1, keepdims=True)
        anat_ref[:, sw + hd * dv:sw + (hd + 1) * dv] = (
            oh * lax.rsqrt(ms + NORM_EPS) * gnorm_ref[:, vs] * gate[:, vs]).astype(BF16)

    s5_block(0)
    for i in range(max(n_blk - 1, GLA_HEADS)):
        if i + 2 < n_blk:
            b_proj(i + 2)
        if i < GLA_HEADS:
            gla_head(i)
        if i + 1 < n_blk:
            s5_block(i + 1)

    y = yperm_ref[...] + dskip_ref[...] * uperm_ref[...]
    y = jax.nn.gelu(y, approximate=True)
    a_perm = y * jax.nn.sigmoid(_dot(y.astype(BF16), wglu_ref[...]) + bglu_ref[...])
    anat_ref[:, 0:sw] = _dot(unperm_ref[...], a_perm.astype(BF16)).astype(BF16)

    o_ref[...] = x + _rms(_dot(anat_ref[...], wout_ref[...]), lnpost_ref[...])


def _s5_discretize(lam_re, lam_im, b_re, b_im, log_dt, power):
    dt = jnp.exp(log_dt)[:, None]
    mag = jnp.exp(lam_re * dt)
    lb_re = mag * jnp.cos(lam_im * dt)
    lb_im = mag * jnp.sin(lam_im * dt)
    inv = 1.0 / (lam_re * lam_re + lam_im * lam_im)
    zr = ((lb_re - 1.0) * lam_re + lb_im * lam_im) * inv
    zi = (lb_im * lam_re - (lb_re - 1.0) * lam_im) * inv
    bb_re = zr[..., None] * b_re - zi[..., None] * b_im
    bb_im = zr[..., None] * b_im + zi[..., None] * b_re
    magp = jnp.exp(lam_re * dt * power)
    lp_re = magp * jnp.cos(lam_im * dt * power)
    lp_im = magp * jnp.sin(lam_im * dt * power)
    return lb_re, lb_im, lp_re, lp_im, bb_re, bb_im


def _even_layer(x, ln_pre, ln_post, w_in, lam_re, lam_im, b_re, b_im, c_re, c_im, d_skip, log_dt,
                w_glu, b_glu, w_alpha_up, b_alpha, gla_norm, w_out):
    bsz, seq, d = x.shape
    groups, state = lam_re.shape
    gs = b_re.shape[-1]
    sw = groups * gs
    rank, kw = w_alpha_up.shape
    vw = gla_norm.shape[0]
    tile = min(EVEN_TILE, seq)
    lg = S5_LANE_GROUPS
    n_blk = groups // lg
    half = lg * state

    lb_re, lb_im, lp_re, lp_im, bb_re, bb_im = _s5_discretize(
        lam_re.astype(F32), lam_im.astype(F32), b_re.astype(F32), b_im.astype(F32),
        log_dt.astype(F32), float(S5_STEPS))

    def lam_pack(re, im):
        out = jnp.zeros((n_blk, SUBLANES, half), F32)
        return out.at[:, 0].set(re.reshape(n_blk, half)).at[:, 1].set(im.reshape(n_blk, half))

    eye = jnp.eye(lg, dtype=F32)

    def b_pack(bb):
        t = bb.reshape(n_blk, lg, state, gs)
        return jnp.einsum('bgpj,gh->bgjhp', t, eye).reshape(n_blk, lg * gs, half)

    def c_pack(cc):
        t = cc.reshape(n_blk, lg, gs, state)
        return jnp.einsum('bgip,gh->bhpgi', t, eye).reshape(n_blk, half, lg * gs)

    bmat = jnp.concatenate([b_pack(bb_re), b_pack(bb_im)], axis=2).astype(BF16)
    cmat = jnp.concatenate([c_pack(c_re.astype(F32)), -c_pack(c_im.astype(F32))], axis=1).astype(BF16)
    main = sw + 2 * kw + 2 * vw
    wa = jnp.zeros((d, LANES), F32).at[:, :rank].set(w_in[:, main:main + rank]).astype(BF16)
    wup = jnp.zeros((LANES, kw), F32).at[:rank].set(w_alpha_up).astype(BF16)
    args = (ln_pre.reshape(1, d), ln_post.reshape(1, d), w_in[:, :main].astype(BF16), wa, wup,
            b_alpha.reshape(1, kw), gla_norm.reshape(1, vw),
            lam_pack(lb_re, lb_im), lam_pack(lp_re, lp_im), bmat, cmat,
            d_skip.reshape(1, sw), w_glu.astype(BF16), b_glu.reshape(1, sw), w_out.astype(BF16))
    rows = tile // S5_STEPS
    r_idx = jnp.arange(tile, dtype=jnp.int32)[:, None]
    c_idx = jnp.arange(tile, dtype=jnp.int32)[None, :]
    unperm = (c_idx == (r_idx % S5_STEPS) * rows + r_idx // S5_STEPS)
    sub = min(GLA_SUB, tile)
    r_idx, c_idx = r_idx[:sub], c_idx[:, :sub]
    in_chunk = (r_idx // GLA_CHUNK == c_idx // GLA_CHUNK)
    causal = (in_chunk & (c_idx <= r_idx)).astype(F32)
    args = args + (causal, causal.astype(BF16), in_chunk.astype(BF16), unperm.astype(BF16))
    scratch = [
        pltpu.VMEM((tile, d), BF16),
        pltpu.VMEM((n_blk, tile, LANES), F32),
        pltpu.VMEM((tile, sw), F32),
        pltpu.VMEM((2, tile, 2 * half), F32),
        pltpu.VMEM((2, tile, 2 * half), F32),
        pltpu.VMEM((tile, sw), F32),
        pltpu.VMEM((rows, 2 * half), F32),
        pltpu.VMEM((rows, 2 * half), F32),
        pltpu.VMEM((tile, sw + vw), BF16),
        pltpu.VMEM((n_blk, SUBLANES, half), F32),
        pltpu.VMEM((tile, vw), F32),
        pltpu.VMEM((GLA_HEADS, vw // GLA_HEADS, kw // GLA_HEADS), F32),
    ]
    return pl.pallas_call(
        _even_kernel,
        grid=(bsz, seq // tile),
        in_specs=[_tile_spec(tile, d)] + [_full_spec(a) for a in args],
        out_specs=_tile_spec(tile, d),
        out_shape=jax.ShapeDtypeStruct(x.shape, x.dtype),
        scratch_shapes=scratch,
        compiler_params=_compiler_params(),
        name="even_mixer",
    )(x, *args)


def kernel(x, ln_mix_pre, ln_mix_post, ln_ffn_pre, ln_ffn_post, ev_w_in, s5_lambda_re, s5_lambda_im, s5_b_re, s5_b_im, s5_c_re, s5_c_im, s5_d, s5_log_dt, s5_w_glu, s5_b_glu, gla_w_alpha_up, gla_b_alpha, gla_norm, ev_w_out, od_w_in, ml_conv_w, ml_conv_b, ml_w_q, ml_w_k, ml_w_v, ml_w_gate, ml_b_gate, ml_norm, ml_skip, od_w_out, ffn_w_up, ffn_conv_w, ffn_conv_b, ffn_w_down):
    depth = ln_mix_pre.shape[0]
    for layer in range(depth):
        if layer % 2 == 0:
            e = layer // 2
            x = _even_layer(x, ln_mix_pre[layer], ln_mix_post[layer], ev_w_in[e], s5_lambda_re[e],
                            s5_lambda_im[e], s5_b_re[e], s5_b_im[e], s5_c_re[e], s5_c_im[e], s5_d[e],
                            s5_log_dt[e], s5_w_glu[e], s5_b_glu[e], gla_w_alpha_up[e], gla_b_alpha[e],
                            gla_norm[e], ev_w_out[e])
        else:
            o = layer // 2
            x = _odd_layer(x, ln_mix_pre[layer], ln_mix_post[layer], od_w_in[o], ml_conv_w[o],
                           ml_conv_b[o], ml_w_q[o], ml_w_k[o], ml_w_v[o], ml_w_gate[o], ml_b_gate[o],
                           ml_norm[o], ml_skip[o], od_w_out[o])
        x = _ffn_layer(x, ln_ffn_pre[layer], ln_ffn_post[layer], ffn_w_up[layer], ffn_conv_w[layer],
                       ffn_conv_b[layer], ffn_w_down[layer])
    return x


def _odd_kernel(x_ref, lnpre_ref, lnpost_ref, win_ref, cw_ref, cb_ref, wqk_ref, wv_ref, wfqk_ref, wfv_ref,
                bg_ref,
                mnorm_ref, skip_ref, wout_ref,
                o_ref,
                h_ref, xc_ref, xcb_ref, xmb_ref, q_ref, k_ref, v_ref, tail_ref, c_ref, n_ref,
                m_ref):
    tile = x_ref.shape[0]
    inner = xc_ref.shape[1]
    heads = c_ref.shape[0]
    dh = inner // heads
    n_blocks = wqk_ref.shape[0]
    bw = wqk_ref.shape[1]
    taps = 4

    @pl.when(pl.program_id(1) == 0)
    def _():
        tail_ref[...] = jnp.zeros_like(tail_ref)
        c_ref[...] = jnp.zeros_like(c_ref)
        n_ref[...] = jnp.zeros_like(n_ref)
        m_ref[...] = jnp.zeros_like(m_ref)

    x = x_ref[...]
    h_ref[...] = _rms(x, lnpre_ref[...]).astype(BF16)

    def conv(xm, w):
        out = w[taps - 1:taps] * xm
        for kk in range(taps - 1):
            out = out + w[kk:kk + 1] * pltpu.roll(xm, taps - 1 - kk, axis=0)
        return out

    gw = 2 * bw
    n_groups = inner // gw

    def in_proj(grp):
        return _dot(h_ref[...], win_ref[:, grp * gw:(grp + 1) * gw])

    def conv_silu(grp, xm):
        ls = slice(grp * gw, (grp + 1) * gw)
        w = cw_ref[:, ls]
        b = cb_ref[:, ls]
        pre = conv(xm, w) + b
        head = jnp.concatenate([tail_ref[:, ls], xm[0:SUBLANES]], axis=0)
        pre_head = (conv(head, w) + b)[SUBLANES:2 * SUBLANES]
        tail_ref[:, ls] = xm[tile - SUBLANES:tile]
        xc = jax.nn.silu(jnp.concatenate([pre_head, pre[SUBLANES:]], axis=0))
        xc_ref[:, ls] = xc
        xcb_ref[:, ls] = xc.astype(BF16)
        xmb_ref[:, ls] = xm.astype(BF16)

    def headwise(grp):
        for blk in range(grp * (gw // bw), (grp + 1) * (gw // bw)):
            ls = slice(blk * bw, (blk + 1) * bw)
            qk = _dot(xcb_ref[:, ls], wqk_ref[blk])
            q_ref[:, ls] = qk[:, :bw].astype(BF16)
            k_ref[:, ls] = qk[:, bw:].astype(BF16)
            v_ref[:, ls] = _dot(xmb_ref[:, ls], wv_ref[blk]).astype(BF16)
        gs = slice(grp * gw, (grp + 1) * gw)
        gate_parts.append(_dot(xcb_ref[:, gs], wfqk_ref[gs, :]) + _dot(xmb_ref[:, gs], wfv_ref[gs, :]))

    gate_parts = []

    xm_next = in_proj(0)
    for grp in range(n_groups):
        xm = xm_next
        if grp + 1 < n_groups:
            xm_next = in_proj(grp + 1)
        if grp > 0:
            headwise(grp - 1)
        conv_silu(grp, xm)
    headwise(n_groups - 1)

    def state_matmuls(hd):
        hs = slice(hd * dh, (hd + 1) * dh)
        qh = q_ref[:, hs]
        c_mat = c_ref[hd]
        return _dot_nt(qh, k_ref[:, hs]), _dot(qh, c_mat.astype(BF16)), c_mat

    ahead = state_matmuls(0)

    gates = bg_ref[...] + functools.reduce(lambda a, b: a + b, gate_parts)
    log_f = _log_sigmoid(gates)
    row = lax.broadcasted_iota(jnp.int32, (tile, tile), 0)
    col = lax.broadcasted_iota(jnp.int32, (tile, tile), 1)
    causal = col <= row
    f_cum = _split_dot(jnp.where(causal, 1.0, 0.0).astype(BF16), log_f)
    gates_t = jnp.transpose(gates)
    f_cum_t = jnp.transpose(f_cum)
    scale = dh ** -0.5

    for hd in range(heads):
        qk_raw, inter_raw, c_mat = ahead
        if hd + 1 < heads:
            ahead = state_matmuls(hd + 1)
        hs = slice(hd * dh, (hd + 1) * dh)
        f_col = f_cum[:, heads + hd:heads + hd + 1]
        f_row = f_cum_t[heads + hd:heads + hd + 1, :]
        i_col = gates[:, hd:hd + 1]
        i_row = gates_t[hd:hd + 1, :]
        m_prev = m_ref[hd:hd + 1, 0:1]
        d_log = jnp.where(causal, f_col - f_row + i_row, -jnp.inf)
        inter_log = f_col + m_prev
        m_loc = jnp.maximum(inter_log, jnp.max(d_log, axis=-1, keepdims=True))
        qh = q_ref[:, hs]
        kh = k_ref[:, hs]
        vh = v_ref[:, hs]
        s = qk_raw * (scale * jnp.exp(d_log - m_loc))
        w_inter = jnp.exp(inter_log - m_loc)
        num = _dot(s.astype(BF16), vh) + w_inter * inter_raw
        o_pre = _dot(h_ref[...], win_ref[:, inner + hd * dh:inner + (hd + 1) * dh])
        n_vec = n_ref[hd:hd + 1, :]
        den = (jnp.sum(s, axis=-1, keepdims=True)
               + w_inter * jnp.sum(qh.astype(F32) * n_vec, axis=-1, keepdims=True))
        denom = jnp.maximum(jnp.abs(den), jnp.exp(-m_loc))
        hh = num * (1.0 / denom)
        ms = jnp.mean(hh * hh, axis=-1, keepdims=True)
        hn = hh * lax.rsqrt(ms + NORM_EPS) * mnorm_ref[:, hs] + skip_ref[:, hs] * xc_ref[:, hs]
        gated = (jax.nn.sigmoid(o_pre) * hn).astype(BF16)
        part = _dot(gated, wout_ref[hs, :])
        mix = part if hd == 0 else mix + part
        f_last = f_col[tile - 1:tile, :]
        w_log = f_last - f_col + i_col
        m_new = jnp.maximum(f_last + m_prev, jnp.max(w_log, axis=0, keepdims=True))
        w_upd = jnp.exp(w_log - m_new)
        decay = jnp.exp(f_last + m_prev - m_new)
        kw = kh.astype(F32) * (scale * w_upd)
        c_ref[hd] = decay * c_mat + _dot_tn(kw.astype(BF16), vh)
        n_ref[hd:hd + 1, :] = decay * n_vec + jnp.sum(kw, axis=0, keepdims=True)
        m_ref[hd:hd + 1, :] = jnp.broadcast_to(m_new, (1, m_ref.shape[1]))

    o_ref[...] = x + _rms(mix, lnpost_ref[...])


def _odd_layer(x, ln_pre, ln_post, w_in, conv_w, conv_b, w_q, w_k, w_v, w_gate, b_gate, m_norm, skip,
               w_out):
    bsz, seq, d = x.shape
    inner = w_out.shape[0]
    heads = b_gate.shape[0] // 2
    tile = min(ODD_TILE, seq)
    bw = MXU_DIM
    n_blocks = inner // bw
    per = bw // MLSTM_QKV_BLOCK
    eye = jnp.eye(per, dtype=F32)

    def blockdiag(w):
        t = w.astype(F32).reshape(n_blocks, per, MLSTM_QKV_BLOCK, MLSTM_QKV_BLOCK)
        return jnp.einsum('bncd,nm->bncmd', t, eye).reshape(n_blocks, bw, bw)

    wqk = jnp.concatenate([blockdiag(w_q), blockdiag(w_k)], axis=2).astype(BF16)
    wv = blockdiag(w_v).astype(BF16)
    nb = inner // MLSTM_QKV_BLOCK
    wg_blocks = w_gate.astype(F32).reshape(3, nb, MLSTM_QKV_BLOCK, 2 * heads)

    def fold(w, g):
        return jnp.einsum('ncd,ndg->ncg', w.astype(F32), g,
                          precision=lax.Precision.HIGHEST).reshape(inner, 2 * heads)

    def pad_lanes(w):
        return jnp.zeros((inner, LANES), F32).at[:, :2 * heads].set(w).astype(BF16)

    wfqk = pad_lanes(fold(w_q, wg_blocks[0]) + fold(w_k, wg_blocks[1]))
    wfv = pad_lanes(fold(w_v, wg_blocks[2]))
    bg = jnp.zeros((1, LANES), F32).at[0, :2 * heads].set(b_gate)
    cw = jnp.zeros((SUBLANES, inner), F32).at[:conv_w.shape[0]].set(conv_w)
    args = (ln_pre.reshape(1, d), ln_post.reshape(1, d), w_in.astype(BF16), cw, conv_b.reshape(1, inner),
            wqk, wv, wfqk, wfv, bg, m_norm.reshape(1, inner), skip.reshape(1, inner), w_out.astype(BF16))
    dh = inner // heads
    scratch = [
        pltpu.VMEM((tile, d), BF16),
        pltpu.VMEM((tile, inner), F32),
        pltpu.VMEM((tile, inner), BF16),
        pltpu.VMEM((tile, inner), BF16),
        pltpu.VMEM((tile, inner), BF16),
        pltpu.VMEM((tile, inner), BF16),
        pltpu.VMEM((tile, inner), BF16),
        pltpu.VMEM((SUBLANES, inner), F32),
        pltpu.VMEM((heads, dh, dh), F32),
        pltpu.VMEM((SUBLANES, dh), F32),
        pltpu.VMEM((SUBLANES, LANES), F32),
    ]
    return pl.pallas_call(
        _odd_kernel,
        grid=(bsz, seq // tile),
        in_specs=[_tile_spec(tile, d)] + [_full_spec(a) for a in args],
        out_specs=_tile_spec(tile, d),
        out_shape=jax.ShapeDtypeStruct(x.shape, x.dtype),
        scratch_shapes=scratch,
        compiler_params=_compiler_params(),
        name="odd_mixer",
    )(x, *args)
```

```python
import functools
import math

import jax
import jax.numpy as jnp
from jax import lax
from jax.experimental import pallas as pl
from jax.experimental.pallas import tpu as pltpu

F32 = jnp.float32
BF16 = jnp.bfloat16

NORM_EPS = 1e-6
DT_UNUSED = None

S5_GROUP_SIZE = 16
S5_STATE = 64
GLA_HEADS = 4
GLA_CHUNK = 64
GLA_GATE_TEMP = 16.0
MLSTM_HEADS = 4
MLSTM_QKV_BLOCK = 4

LANES = 128
SUBLANES = 8
MXU_DIM = 256
VMEM_LIMIT_BYTES = 58 * 1024 * 1024

EVEN_TILE = 512
S5_STEPS = 32
ODD_TILE = 256
FFN_TILE = 1024
FFN_COLS = 256
S5_LANE_GROUPS = LANES // S5_GROUP_SIZE
GLA_SUB = EVEN_TILE


def _dot(a, b):
    return jnp.dot(a, b, preferred_element_type=F32)


def _dot_nt(a, b):
    return lax.dot_general(a, b, (((1,), (1,)), ((), ())), preferred_element_type=F32)


def _dot_tn(a, b):
    return lax.dot_general(a, b, (((0,), (0,)), ((), ())), preferred_element_type=F32)


def _rms(x, g):
    ms = jnp.mean(x * x, axis=-1, keepdims=True)
    return x * lax.rsqrt(ms + NORM_EPS) * g


def _log_sigmoid(x):
    return jnp.minimum(x, 0.0) - jnp.log1p(jnp.exp(-jnp.abs(x)))


def _split_dot(mask_bf16, x):
    hi = x.astype(BF16)
    lo = (x - hi.astype(F32)).astype(BF16)
    return _dot(mask_bf16, hi) + _dot(mask_bf16, lo)


def _full_spec(arr):
    nd = arr.ndim
    return pl.BlockSpec(arr.shape, lambda b, l, _nd=nd: (0,) * _nd,
                        pipeline_mode=pl.Buffered(1))


def _tile_spec(tile, d):
    return pl.BlockSpec((None, tile, d), lambda b, l: (b, l, 0))


def _compiler_params():
    return pltpu.CompilerParams(dimension_semantics=("arbitrary", "arbitrary"),
                                vmem_limit_bytes=VMEM_LIMIT_BYTES)


def _ffn_kernel(x_ref, lnpre_ref, lnpost_ref, wup_ref, cw_ref, cb_ref, wd_ref,
                o_ref, h_ref, act_ref, tail_ref):
    tile = x_ref.shape[0]
    f = wd_ref.shape[0]
    n_chunks = tail_ref.shape[0]
    fc = tail_ref.shape[2]

    @pl.when(pl.program_id(1) == 0)
    def _():
        tail_ref[...] = jnp.zeros_like(tail_ref)

    x = x_ref[...]
    h_ref[...] = _rms(x, lnpre_ref[...]).astype(BF16)

    def conv3(g, w):
        return (w[0:1] * pltpu.roll(g, 2, axis=0) + w[1:2] * pltpu.roll(g, 1, axis=0)
                + w[2:3] * g)

    def up(c):
        h = h_ref[...]
        return (_dot(h, wup_ref[:, c * fc:(c + 1) * fc]),
                _dot(h, wup_ref[:, f + c * fc:f + (c + 1) * fc]))

    def gate(c, g, u):
        w = cw_ref[:, c * fc:(c + 1) * fc]
        b = cb_ref[:, c * fc:(c + 1) * fc]
        conv = conv3(g, w) + b
        head = jnp.concatenate([tail_ref[c], g[0:SUBLANES]], axis=0)
        conv_head = (conv3(head, w) + b)[SUBLANES:2 * SUBLANES]
        tail_ref[c] = g[tile - SUBLANES:tile]
        conv = jnp.concatenate([conv_head, conv[SUBLANES:]], axis=0)
        act_ref[:, c * fc:(c + 1) * fc] = (jax.nn.gelu(conv, approximate=True) * u).astype(BF16)

    gu = up(0)
    for c in range(n_chunks):
        gu_next = up(c + 1) if c + 1 < n_chunks else None
        gate(c, *gu)
        gu = gu_next
    o_ref[...] = x + _rms(_dot(act_ref[...], wd_ref[...]), lnpost_ref[...])


def _ffn_layer(x, ln_pre, ln_post, w_up, conv_w, conv_b, w_down):
    bsz, seq, d = x.shape
    f = w_down.shape[0]
    tile = min(FFN_TILE, seq)
    nc = f // FFN_COLS
    taps = conv_w.shape[0]
    cw = jnp.zeros((SUBLANES, f), F32).at[:taps].set(conv_w)
    args = (ln_pre.reshape(1, d), ln_post.reshape(1, d), w_up.astype(BF16), cw, conv_b.reshape(1, f),
            w_down.astype(BF16))
    return pl.pallas_call(
        _ffn_kernel,
        grid=(bsz, seq // tile),
        in_specs=[_tile_spec(tile, d)] + [_full_spec(a) for a in args],
        out_specs=_tile_spec(tile, d),
        out_shape=jax.ShapeDtypeStruct(x.shape, x.dtype),
        scratch_shapes=[pltpu.VMEM((tile, d), BF16), pltpu.VMEM((tile, f), BF16),
                        pltpu.VMEM((nc, SUBLANES, FFN_COLS), F32)],
        compiler_params=_compiler_params(),
        name="conv_ffn",
    )(x, *args)


def _even_kernel(x_ref, lnpre_ref, lnpost_ref, wmain_ref, wa_ref, wup_ref, balpha_ref, gnorm_ref,
                 lam_ref, lamc_ref, bmat_ref, cmat_ref, dskip_ref, wglu_ref, bglu_ref, wout_ref,
                 causalf_ref, causalb_ref, inchunk_ref, unperm_ref,
                 o_ref,
                 h_ref, unat_ref, uperm_ref, bu_ref, xs_ref, yperm_ref, e_ref, cin_ref, anat_ref,
                 s5carry_ref, ogla_ref, glastate_ref):
    tile = x_ref.shape[0]
    n_blk = unat_ref.shape[0]
    sw = n_blk * LANES
    half = bu_ref.shape[2] // 2
    steps = S5_STEPS
    rows = tile // steps
    kw = wup_ref.shape[1]
    vw = ogla_ref.shape[1]
    dk = kw // GLA_HEADS
    dv = vw // GLA_HEADS
    n_chunks = tile // GLA_CHUNK

    @pl.when(pl.program_id(1) == 0)
    def _():
        s5carry_ref[...] = jnp.zeros_like(s5carry_ref)
        glastate_ref[...] = jnp.zeros_like(glastate_ref)

    x = x_ref[...]
    h_ref[...] = _rms(x, lnpre_ref[...]).astype(BF16)
    h = h_ref[...]

    u = _dot(h, wmain_ref[:, 0:sw])
    for blk in range(n_blk):
        unat_ref[blk] = u[:, blk * LANES:(blk + 1) * LANES]
    for blk in range(n_blk):
        for s in range(steps):
            uperm_ref[s * rows:(s + 1) * rows, blk * LANES:(blk + 1) * LANES] = (
                unat_ref[blk, pl.ds(s, rows, stride=steps), :])

    def b_proj(blk):
        bu_ref[blk % 2] = _dot(uperm_ref[:, blk * LANES:(blk + 1) * LANES].astype(BF16),
                               bmat_ref[blk])

    def s5_block(blk):
        lanes = slice(blk * LANES, (blk + 1) * LANES)
        slot = blk % 2
        a_re = lam_ref[blk, 0:1, :]
        a_im = lam_ref[blk, 1:2, :]

        def advance(s, state, store):
            s_re, s_im = state
            b = bu_ref[slot, s * rows:(s + 1) * rows, :]
            n_re = a_re * s_re - a_im * s_im + b[:, :half]
            n_im = a_re * s_im + a_im * s_re + b[:, half:]
            if store:
                xs_ref[slot, s * rows:(s + 1) * rows, :] = jnp.concatenate([n_re, n_im], axis=1)
            return n_re, n_im

        state = (jnp.zeros((rows, half), F32), jnp.zeros((rows, half), F32))
        for s in range(steps):
            state = advance(s, state, False)
        e_ref[...] = jnp.concatenate(state, axis=1)
        ac_re = lamc_ref[blk, 0:1, :]
        ac_im = lamc_ref[blk, 1:2, :]
        c_re = s5carry_ref[blk, 0:1, :]
        c_im = s5carry_ref[blk, 1:2, :]
        for i in range(rows):
            cin_ref[i:i + 1, :] = jnp.concatenate([c_re, c_im], axis=1)
            e_row = e_ref[i:i + 1, :]
            c_re, c_im = (ac_re * c_re - ac_im * c_im + e_row[:, :half],
                          ac_re * c_im + ac_im * c_re + e_row[:, half:])
        s5carry_ref[blk, 0:1, :] = c_re
        s5carry_ref[blk, 1:2, :] = c_im
        cin = cin_ref[...]
        state = (cin[:, :half], cin[:, half:])
        for s in range(steps):
            state = advance(s, state, True)
        yperm_ref[:, lanes] = _dot(xs_ref[slot].astype(BF16), cmat_ref[blk])

    for blk in range(min(2, n_blk)):
        b_proj(blk)

    o0 = sw
    r0 = o0 + 2 * kw + vw
    q = _dot(h, wmain_ref[:, o0:o0 + kw]) * (dk ** -0.5)
    k = _dot(h, wmain_ref[:, o0 + kw:o0 + 2 * kw])
    v = _dot(h, wmain_ref[:, o0 + 2 * kw:o0 + 2 * kw + vw]).astype(BF16)
    a_lr = _dot(h, wa_ref[...])
    z = _dot(a_lr.astype(BF16), wup_ref[...]) + balpha_ref[...]
    log_a = _log_sigmoid(z) * (1.0 / GLA_GATE_TEMP)

    sub = causalf_ref.shape[0]
    subs = [slice(i * sub, (i + 1) * sub) for i in range(tile // sub)]
    g = jnp.concatenate([_split_dot(causalb_ref[...], log_a[rs]) for rs in subs], axis=0)
    g_last = jnp.concatenate([_split_dot(inchunk_ref[...], log_a[rs]) for rs in subs], axis=0)
    q_dec = (q * jnp.exp(g)).astype(BF16)
    k_dec = (k * jnp.exp(-g)).astype(BF16)
    k_upd = (k * jnp.exp(g_last - g)).astype(BF16)
    decay = jnp.exp(g_last)
    keep = causalf_ref[...] > 0.5
    gate = jax.nn.silu(_dot(h, wmain_ref[:, r0:r0 + vw]))

    def gla_head(hd):
        ks = slice(hd * dk, (hd + 1) * dk)
        vs = slice(hd * dv, (hd + 1) * dv)
        qh, kh, kuh, vh = q_dec[:, ks], k_dec[:, ks], k_upd[:, ks], v[:, vs]
        o_intra = jnp.concatenate(
            [_dot(jnp.where(keep, _dot_nt(qh[rs], kh[rs]), 0.0).astype(BF16), vh[rs]) for rs in subs],
            axis=0)
        chunks = [slice(c * GLA_CHUNK, (c + 1) * GLA_CHUNK) for c in range(n_chunks)]
        d_st = [_dot_tn(vh[rs], kuh[rs]) for rs in chunks]
        st = glastate_ref[hd]
        states = []
        for c in range(n_chunks):
            states.append(st.astype(BF16))
            st = st * decay[c * GLA_CHUNK:c * GLA_CHUNK + 1, ks] + d_st[c]
        glastate_ref[hd] = st
        for c, rs in enumerate(chunks):
            ogla_ref[rs, vs] = o_intra[rs] + _dot_nt(qh[rs], states[c])
        oh = ogla_ref[:, vs]
        ms = jnp.mean(oh * oh, axis=-1, keepdims=True)
        anat_ref[:, sw + hd * dv:sw + (hd + 1) * dv] = (
            oh * lax.rsqrt(ms + NORM_EPS) * gnorm_ref[:, vs] * gate[:, vs]).astype(BF16)

    s5_block(0)
    for i in range(max(n_blk - 1, GLA_HEADS)):
        if i + 2 < n_blk:
            b_proj(i + 2)
        if i < GLA_HEADS:
            gla_head(i)
        if i + 1 < n_blk:
            s5_block(i + 1)

    y = yperm_ref[...] + dskip_ref[...] * uperm_ref[...]
    y = jax.nn.gelu(y, approximate=True)
    a_perm = y * jax.nn.sigmoid(_dot(y.astype(BF16), wglu_ref[...]) + bglu_ref[...])
    anat_ref[:, 0:sw] = _dot(unperm_ref[...], a_perm.astype(BF16)).astype(BF16)

    o_ref[...] = x + _rms(_dot(anat_ref[...], wout_ref[...]), lnpost_ref[...])


def _s5_discretize(lam_re, lam_im, b_re, b_im, log_dt, power):
    dt = jnp.exp(log_dt)[:, None]
    mag = jnp.exp(lam_re * dt)
    lb_re = mag * jnp.cos(lam_im * dt)
    lb_im = mag * jnp.sin(lam_im * dt)
    inv = 1.0 / (lam_re * lam_re + lam_im * lam_im)
    zr = ((lb_re - 1.0) * lam_re + lb_im * lam_im) * inv
    zi = (lb_im * lam_re - (lb_re - 1.0) * lam_im) * inv
    bb_re = zr[..., None] * b_re - zi[..., None] * b_im
    bb_im = zr[..., None] * b_im + zi[..., None] * b_re
    magp = jnp.exp(lam_re * dt * power)
    lp_re = magp * jnp.cos(lam_im * dt * power)
    lp_im = magp * jnp.sin(lam_im * dt * power)
    return lb_re, lb_im, lp_re, lp_im, bb_re, bb_im


def _even_layer(x, ln_pre, ln_post, w_in, lam_re, lam_im, b_re, b_im, c_re, c_im, d_skip, log_dt,
                w_glu, b_glu, w_alpha_up, b_alpha, gla_norm, w_out):
    bsz, seq, d = x.shape
    groups, state = lam_re.shape
    gs = b_re.shape[-1]
    sw = groups * gs
    rank, kw = w_alpha_up.shape
    vw = gla_norm.shape[0]
    tile = min(EVEN_TILE, seq)
    lg = S5_LANE_GROUPS
    n_blk = groups // lg
    half = lg * state

    lb_re, lb_im, lp_re, lp_im, bb_re, bb_im = _s5_discretize(
        lam_re.astype(F32), lam_im.astype(F32), b_re.astype(F32), b_im.astype(F32),
        log_dt.astype(F32), float(S5_STEPS))

    def lam_pack(re, im):
        out = jnp.zeros((n_blk, SUBLANES, half), F32)
        return out.at[:, 0].set(re.reshape(n_blk, half)).at[:, 1].set(im.reshape(n_blk, half))

    same_group = (jnp.arange(lg * gs)[:, None] // gs) == (jnp.arange(half)[None, :] // state)

    def b_pack(bb):
        rows = bb.transpose(0, 2, 1).reshape(n_blk, lg * gs, state)
        return jnp.where(same_group, jnp.tile(rows, (1, 1, lg)), 0.0)

    def c_pack(cc):
        rows = cc.transpose(0, 2, 1).reshape(n_blk, half, gs)
        return jnp.where(same_group.T, jnp.tile(rows, (1, 1, lg)), 0.0)

    bmat = jnp.concatenate([b_pack(bb_re), b_pack(bb_im)], axis=2).astype(BF16)
    cmat = jnp.concatenate([c_pack(c_re.astype(F32)), -c_pack(c_im.astype(F32))], axis=1).astype(BF16)
    main = sw + 2 * kw + 2 * vw
    wa = jnp.zeros((d, LANES), F32).at[:, :rank].set(w_in[:, main:main + rank]).astype(BF16)
    wup = jnp.zeros((LANES, kw), F32).at[:rank].set(w_alpha_up).astype(BF16)
    args = (ln_pre.reshape(1, d), ln_post.reshape(1, d), w_in[:, :main].astype(BF16), wa, wup,
            b_alpha.reshape(1, kw), gla_norm.reshape(1, vw),
            lam_pack(lb_re, lb_im), lam_pack(lp_re, lp_im), bmat, cmat,
            d_skip.reshape(1, sw), w_glu.astype(BF16), b_glu.reshape(1, sw), w_out.astype(BF16))
    rows = tile // S5_STEPS
    r_idx = jnp.arange(tile, dtype=jnp.int32)[:, None]
    c_idx = jnp.arange(tile, dtype=jnp.int32)[None, :]
    unperm = (c_idx == (r_idx % S5_STEPS) * rows + r_idx // S5_STEPS)
    sub = min(GLA_SUB, tile)
    r_idx, c_idx = r_idx[:sub], c_idx[:, :sub]
    in_chunk = (r_idx // GLA_CHUNK == c_idx // GLA_CHUNK)
    causal = (in_chunk & (c_idx <= r_idx)).astype(F32)
    args = args + (causal, causal.astype(BF16), in_chunk.astype(BF16), unperm.astype(BF16))
    scratch = [
        pltpu.VMEM((tile, d), BF16),
        pltpu.VMEM((n_blk, tile, LANES), F32),
        pltpu.VMEM((tile, sw), F32),
        pltpu.VMEM((2, tile, 2 * half), F32),
        pltpu.VMEM((2, tile, 2 * half), F32),
        pltpu.VMEM((tile, sw), F32),
        pltpu.VMEM((rows, 2 * half), F32),
        pltpu.VMEM((rows, 2 * half), F32),
        pltpu.VMEM((tile, sw + vw), BF16),
        pltpu.VMEM((n_blk, SUBLANES, half), F32),
        pltpu.VMEM((tile, vw), F32),
        pltpu.VMEM((GLA_HEADS, vw // GLA_HEADS, kw // GLA_HEADS), F32),
    ]
    return pl.pallas_call(
        _even_kernel,
        grid=(bsz, seq // tile),
        in_specs=[_tile_spec(tile, d)] + [_full_spec(a) for a in args],
        out_specs=_tile_spec(tile, d),
        out_shape=jax.ShapeDtypeStruct(x.shape, x.dtype),
        scratch_shapes=scratch,
        compiler_params=_compiler_params(),
        name="even_mixer",
    )(x, *args)


def kernel(x, ln_mix_pre, ln_mix_post, ln_ffn_pre, ln_ffn_post, ev_w_in, s5_lambda_re, s5_lambda_im, s5_b_re, s5_b_im, s5_c_re, s5_c_im, s5_d, s5_log_dt, s5_w_glu, s5_b_glu, gla_w_alpha_up, gla_b_alpha, gla_norm, ev_w_out, od_w_in, ml_conv_w, ml_conv_b, ml_w_q, ml_w_k, ml_w_v, ml_w_gate, ml_b_gate, ml_norm, ml_skip, od_w_out, ffn_w_up, ffn_conv_w, ffn_conv_b, ffn_w_down):
    depth = ln_mix_pre.shape[0]
    for layer in range(depth):
        if layer % 2 == 0:
            e = layer // 2
            x = _even_layer(x, ln_mix_pre[layer], ln_mix_post[layer], ev_w_in[e], s5_lambda_re[e],
                            s5_lambda_im[e], s5_b_re[e], s5_b_im[e], s5_c_re[e], s5_c_im[e], s5_d[e],
                            s5_log_dt[e], s5_w_glu[e], s5_b_glu[e], gla_w_alpha_up[e], gla_b_alpha[e],
                            gla_norm[e], ev_w_out[e])
        else:
            o = layer // 2
            x = _odd_layer(x, ln_mix_pre[layer], ln_mix_post[layer], od_w_in[o], ml_conv_w[o],
                           ml_conv_b[o], ml_w_q[o], ml_w_k[o], ml_w_v[o], ml_w_gate[o], ml_b_gate[o],
                           ml_norm[o], ml_skip[o], od_w_out[o])
        x = _ffn_layer(x, ln_ffn_pre[layer], ln_ffn_post[layer], ffn_w_up[layer], ffn_conv_w[layer],
                       ffn_conv_b[layer], ffn_w_down[layer])
    return x


def _odd_kernel(x_ref, lnpre_ref, lnpost_ref, win_ref, cw_ref, cb_ref, wqk_ref, wv_ref, wfqk_ref, wfv_ref,
                bg_ref,
                mnorm_ref, skip_ref, wout_ref,
                o_ref,
                h_ref, xc_ref, xcb_ref, xmb_ref, q_ref, k_ref, v_ref, tail_ref, c_ref, n_ref,
                m_ref):
    tile = x_ref.shape[0]
    inner = xc_ref.shape[1]
    heads = c_ref.shape[0]
    dh = inner // heads
    n_blocks = wqk_ref.shape[0]
    bw = wqk_ref.shape[1]
    taps = 4

    @pl.when(pl.program_id(1) == 0)
    def _():
        tail_ref[...] = jnp.zeros_like(tail_ref)
        c_ref[...] = jnp.zeros_like(c_ref)
        n_ref[...] = jnp.zeros_like(n_ref)
        m_ref[...] = jnp.zeros_like(m_ref)

    x = x_ref[...]
    h_ref[...] = _rms(x, lnpre_ref[...]).astype(BF16)

    def conv(xm, w):
        out = w[taps - 1:taps] * xm
        for kk in range(taps - 1):
            out = out + w[kk:kk + 1] * pltpu.roll(xm, taps - 1 - kk, axis=0)
        return out

    gw = 2 * bw
    n_groups = inner // gw

    def in_proj(grp):
        return _dot(h_ref[...], win_ref[:, grp * gw:(grp + 1) * gw])

    def conv_silu(grp, xm):
        ls = slice(grp * gw, (grp + 1) * gw)
        w = cw_ref[:, ls]
        b = cb_ref[:, ls]
        pre = conv(xm, w) + b
        head = jnp.concatenate([tail_ref[:, ls], xm[0:SUBLANES]], axis=0)
        pre_head = (conv(head, w) + b)[SUBLANES:2 * SUBLANES]
        tail_ref[:, ls] = xm[tile - SUBLANES:tile]
        xc = jax.nn.silu(jnp.concatenate([pre_head, pre[SUBLANES:]], axis=0))
        xc_ref[:, ls] = xc
        xcb_ref[:, ls] = xc.astype(BF16)
        xmb_ref[:, ls] = xm.astype(BF16)

    def headwise(grp):
        for blk in range(grp * (gw // bw), (grp + 1) * (gw // bw)):
            ls = slice(blk * bw, (blk + 1) * bw)
            qk = _dot(xcb_ref[:, ls], wqk_ref[blk])
            q_ref[:, ls] = qk[:, :bw].astype(BF16)
            k_ref[:, ls] = qk[:, bw:].astype(BF16)
            v_ref[:, ls] = _dot(xmb_ref[:, ls], wv_ref[blk]).astype(BF16)
        gs = slice(grp * gw, (grp + 1) * gw)
        gate_parts.append(_dot(xcb_ref[:, gs], wfqk_ref[gs, :]) + _dot(xmb_ref[:, gs], wfv_ref[gs, :]))

    gate_parts = []

    xm_next = in_proj(0)
    for grp in range(n_groups):
        xm = xm_next
        if grp + 1 < n_groups:
            xm_next = in_proj(grp + 1)
        if grp > 0:
            headwise(grp - 1)
        conv_silu(grp, xm)
    headwise(n_groups - 1)

    def state_matmuls(hd):
        hs = slice(hd * dh, (hd + 1) * dh)
        qh = q_ref[:, hs]
        c_mat = c_ref[hd]
        return _dot_nt(qh, k_ref[:, hs]), _dot(qh, c_mat.astype(BF16)), c_mat

    ahead = state_matmuls(0)

    gates = bg_ref[...] + functools.reduce(lambda a, b: a + b, gate_parts)
    log_f = _log_sigmoid(gates)
    row = lax.broadcasted_iota(jnp.int32, (tile, tile), 0)
    col = lax.broadcasted_iota(jnp.int32, (tile, tile), 1)
    causal = col <= row
    f_cum = _split_dot(jnp.where(causal, 1.0, 0.0).astype(BF16), log_f)
    gates_t = jnp.transpose(gates)
    f_cum_t = jnp.transpose(f_cum)
    scale = dh ** -0.5

    for hd in range(heads):
        qk_raw, inter_raw, c_mat = ahead
        if hd + 1 < heads:
            ahead = state_matmuls(hd + 1)
        hs = slice(hd * dh, (hd + 1) * dh)
        f_col = f_cum[:, heads + hd:heads + hd + 1]
        f_row = f_cum_t[heads + hd:heads + hd + 1, :]
        i_col = gates[:, hd:hd + 1]
        i_row = gates_t[hd:hd + 1, :]
        m_prev = m_ref[hd:hd + 1, 0:1]
        d_log = jnp.where(causal, f_col - f_row + i_row, -jnp.inf)
        inter_log = f_col + m_prev
        m_loc = jnp.maximum(inter_log, jnp.max(d_log, axis=-1, keepdims=True))
        qh = q_ref[:, hs]
        kh = k_ref[:, hs]
        vh = v_ref[:, hs]
        s = qk_raw * (scale * jnp.exp(d_log - m_loc))
        w_inter = jnp.exp(inter_log - m_loc)
        num = _dot(s.astype(BF16), vh) + w_inter * inter_raw
        o_pre = _dot(h_ref[...], win_ref[:, inner + hd * dh:inner + (hd + 1) * dh])
        n_vec = n_ref[hd:hd + 1, :]
        den = (jnp.sum(s, axis=-1, keepdims=True)
               + w_inter * jnp.sum(qh.astype(F32) * n_vec, axis=-1, keepdims=True))
        denom = jnp.maximum(jnp.abs(den), jnp.exp(-m_loc))
        hh = num * (1.0 / denom)
        ms = jnp.mean(hh * hh, axis=-1, keepdims=True)
        hn = hh * lax.rsqrt(ms + NORM_EPS) * mnorm_ref[:, hs] + skip_ref[:, hs] * xc_ref[:, hs]
        gated = (jax.nn.sigmoid(o_pre) * hn).astype(BF16)
        part = _dot(gated, wout_ref[hs, :])
        mix = part if hd == 0 else mix + part
        f_last = f_col[tile - 1:tile, :]
        w_log = f_last - f_col + i_col
        m_new = jnp.maximum(f_last + m_prev, jnp.max(w_log, axis=0, keepdims=True))
        w_upd = jnp.exp(w_log - m_new)
        decay = jnp.exp(f_last + m_prev - m_new)
        kw = kh.astype(F32) * (scale * w_upd)
        c_ref[hd] = decay * c_mat + _dot_tn(kw.astype(BF16), vh)
        n_ref[hd:hd + 1, :] = decay * n_vec + jnp.sum(kw, axis=0, keepdims=True)
        m_ref[hd:hd + 1, :] = jnp.broadcast_to(m_new, (1, m_ref.shape[1]))

    o_ref[...] = x + _rms(mix, lnpost_ref[...])


def _odd_layer(x, ln_pre, ln_post, w_in, conv_w, conv_b, w_q, w_k, w_v, w_gate, b_gate, m_norm, skip,
               w_out):
    bsz, seq, d = x.shape
    inner = w_out.shape[0]
    heads = b_gate.shape[0] // 2
    tile = min(ODD_TILE, seq)
    bw = MXU_DIM
    n_blocks = inner // bw
    per = bw // MLSTM_QKV_BLOCK
    on_diag = (jnp.arange(bw)[:, None] // MLSTM_QKV_BLOCK) == (jnp.arange(bw)[None, :] // MLSTM_QKV_BLOCK)

    def blockdiag(w):
        rows = w.astype(F32).reshape(n_blocks, bw, MLSTM_QKV_BLOCK)
        return jnp.where(on_diag, jnp.tile(rows, (1, 1, per)), 0.0)

    wqk = jnp.concatenate([blockdiag(w_q), blockdiag(w_k)], axis=2).astype(BF16)
    wv = blockdiag(w_v).astype(BF16)
    nb = inner // MLSTM_QKV_BLOCK
    wg_blocks = w_gate.astype(F32).reshape(3, nb, MLSTM_QKV_BLOCK, 2 * heads)

    def fold(w, g):
        return jnp.einsum('ncd,ndg->ncg', w.astype(F32), g,
                          precision=lax.Precision.HIGHEST).reshape(inner, 2 * heads)

    def pad_lanes(w):
        return jnp.zeros((inner, LANES), F32).at[:, :2 * heads].set(w).astype(BF16)

    wfqk = pad_lanes(fold(w_q, wg_blocks[0]) + fold(w_k, wg_blocks[1]))
    wfv = pad_lanes(fold(w_v, wg_blocks[2]))
    bg = jnp.zeros((1, LANES), F32).at[0, :2 * heads].set(b_gate)
    cw = jnp.zeros((SUBLANES, inner), F32).at[:conv_w.shape[0]].set(conv_w)
    args = (ln_pre.reshape(1, d), ln_post.reshape(1, d), w_in.astype(BF16), cw, conv_b.reshape(1, inner),
            wqk, wv, wfqk, wfv, bg, m_norm.reshape(1, inner), skip.reshape(1, inner), w_out.astype(BF16))
    dh = inner // heads
    scratch = [
        pltpu.VMEM((tile, d), BF16),
        pltpu.VMEM((tile, inner), F32),
        pltpu.VMEM((tile, inner), BF16),
        pltpu.VMEM((tile, inner), BF16),
        pltpu.VMEM((tile, inner), BF16),
        pltpu.VMEM((tile, inner), BF16),
        pltpu.VMEM((tile, inner), BF16),
        pltpu.VMEM((SUBLANES, inner), F32),
        pltpu.VMEM((heads, dh, dh), F32),
        pltpu.VMEM((SUBLANES, dh), F32),
        pltpu.VMEM((SUBLANES, LANES), F32),
    ]
    return pl.pallas_call(
        _odd_kernel,
        grid=(bsz, seq // tile),
        in_specs=[_tile_spec(tile, d)] + [_full_spec(a) for a in args],
        out_specs=_tile_spec(tile, d),
        out_shape=jax.ShapeDtypeStruct(x.shape, x.dtype),
        scratch_shapes=scratch,
        compiler_params=_compiler_params(),
        name="odd_mixer",
    )(x, *args)
```

```python
import functools

import jax
import jax.numpy as jnp
from jax import lax
from jax.experimental import pallas as pl
from jax.experimental.pallas import tpu as pltpu

F32 = jnp.float32
BF16 = jnp.bfloat16

NORM_EPS = 1e-6

S5_GROUP_SIZE = 16
GLA_HEADS = 4
GLA_CHUNK = 64
GLA_GATE_TEMP = 16.0
MLSTM_QKV_BLOCK = 4
MLSTM_CONV_TAPS = 4

LANES = 128
SUBLANES = 8
MXU_DIM = 256
VMEM_LIMIT_BYTES = 58 * 1024 * 1024

EVEN_TILE = 512
S5_STEPS = 32
ODD_TILE = 512
ODD_CHUNK = 256
FFN_TILE = 1024
FFN_COLS = 256
S5_LANE_GROUPS = LANES // S5_GROUP_SIZE
GLA_SUB = EVEN_TILE


def _dot(a, b):
    return jnp.dot(a, b, preferred_element_type=F32)


def _dot_nt(a, b):
    return lax.dot_general(a, b, (((1,), (1,)), ((), ())), preferred_element_type=F32)


def _dot_tn(a, b):
    return lax.dot_general(a, b, (((0,), (0,)), ((), ())), preferred_element_type=F32)


def _rms(x, g):
    ms = jnp.mean(x * x, axis=-1, keepdims=True)
    return x * lax.rsqrt(ms + NORM_EPS) * g


def _log_sigmoid(x):
    return jnp.minimum(x, 0.0) - jnp.log1p(jnp.exp(-jnp.abs(x)))


def _split_dot(mask_bf16, x):
    hi = x.astype(BF16)
    lo = (x - hi.astype(F32)).astype(BF16)
    return _dot(mask_bf16, hi) + _dot(mask_bf16, lo)


def _full_spec(arr):
    nd = arr.ndim
    return pl.BlockSpec(arr.shape, lambda b, l, _nd=nd: (0,) * _nd,
                        pipeline_mode=pl.Buffered(1))


def _tile_spec(tile, d):
    return pl.BlockSpec((None, tile, d), lambda b, l: (b, l, 0))


def _compiler_params():
    return pltpu.CompilerParams(dimension_semantics=("arbitrary", "arbitrary"),
                                vmem_limit_bytes=VMEM_LIMIT_BYTES)


def _ffn_kernel(x_ref, lnpre_ref, lnpost_ref, wup_ref, cw_ref, cb_ref, wd_ref,
                o_ref, h_ref, act_ref, tail_ref):
    tile = x_ref.shape[0]
    f = wd_ref.shape[0]
    n_chunks = tail_ref.shape[0]
    fc = tail_ref.shape[2]

    @pl.when(pl.program_id(1) == 0)
    def _():
        tail_ref[...] = jnp.zeros_like(tail_ref)

    x = x_ref[...]
    h_ref[...] = _rms(x, lnpre_ref[...]).astype(BF16)

    def conv3(g, w):
        return (w[0:1] * pltpu.roll(g, 2, axis=0) + w[1:2] * pltpu.roll(g, 1, axis=0)
                + w[2:3] * g)

    def up(c):
        h = h_ref[...]
        return (_dot(h, wup_ref[:, c * fc:(c + 1) * fc]),
                _dot(h, wup_ref[:, f + c * fc:f + (c + 1) * fc]))

    def gate(c, g, u):
        w = cw_ref[:, c * fc:(c + 1) * fc]
        b = cb_ref[:, c * fc:(c + 1) * fc]
        conv = conv3(g, w) + b
        head = jnp.concatenate([tail_ref[c], g[0:SUBLANES]], axis=0)
        conv_head = (conv3(head, w) + b)[SUBLANES:2 * SUBLANES]
        tail_ref[c] = g[tile - SUBLANES:tile]
        conv = jnp.concatenate([conv_head, conv[SUBLANES:]], axis=0)
        act_ref[:, c * fc:(c + 1) * fc] = (jax.nn.gelu(conv, approximate=True) * u).astype(BF16)

    gu = up(0)
    for c in range(n_chunks):
        gu_next = up(c + 1) if c + 1 < n_chunks else None
        gate(c, *gu)
        gu = gu_next
    o_ref[...] = x + _rms(_dot(act_ref[...], wd_ref[...]), lnpost_ref[...])


def _ffn_layer(x, ln_pre, ln_post, w_up, conv_w, conv_b, w_down):
    bsz, seq, d = x.shape
    f = w_down.shape[0]
    tile = min(FFN_TILE, seq)
    nc = f // FFN_COLS
    taps = conv_w.shape[0]
    cw = jnp.zeros((SUBLANES, f), F32).at[:taps].set(conv_w)
    args = (ln_pre.reshape(1, d), ln_post.reshape(1, d), w_up.astype(BF16), cw, conv_b.reshape(1, f),
            w_down.astype(BF16))
    return pl.pallas_call(
        _ffn_kernel,
        grid=(bsz, seq // tile),
        in_specs=[_tile_spec(tile, d)] + [_full_spec(a) for a in args],
        out_specs=_tile_spec(tile, d),
        out_shape=jax.ShapeDtypeStruct(x.shape, x.dtype),
        scratch_shapes=[pltpu.VMEM((tile, d), BF16), pltpu.VMEM((tile, f), BF16),
                        pltpu.VMEM((nc, SUBLANES, FFN_COLS), F32)],
        compiler_params=_compiler_params(),
        name="conv_ffn",
    )(x, *args)


def _even_kernel(x_ref, lnpre_ref, lnpost_ref, wmain_ref, wa_ref, wup_ref, balpha_ref, gnorm_ref,
                 lam_ref, lamc_ref, bmat_ref, cmat_ref, dskip_ref, wglu_ref, bglu_ref, wout_ref,
                 causalf_ref, causalb_ref, inchunk_ref, unperm_ref,
                 o_ref,
                 h_ref, unat_ref, uperm_ref, bu_ref, xs_ref, yperm_ref, e_ref, cin_ref, anat_ref,
                 s5carry_ref, ogla_ref, glastate_ref):
    tile = x_ref.shape[0]
    n_blk = unat_ref.shape[0]
    sw = n_blk * LANES
    half = bu_ref.shape[2] // 2
    steps = S5_STEPS
    rows = tile // steps
    kw = wup_ref.shape[1]
    vw = ogla_ref.shape[1]
    dk = kw // GLA_HEADS
    dv = vw // GLA_HEADS
    n_chunks = tile // GLA_CHUNK

    @pl.when(pl.program_id(1) == 0)
    def _():
        s5carry_ref[...] = jnp.zeros_like(s5carry_ref)
        glastate_ref[...] = jnp.zeros_like(glastate_ref)

    x = x_ref[...]
    h_ref[...] = _rms(x, lnpre_ref[...]).astype(BF16)
    h = h_ref[...]

    u = _dot(h, wmain_ref[:, 0:sw])
    for blk in range(n_blk):
        unat_ref[blk] = u[:, blk * LANES:(blk + 1) * LANES]
    for blk in range(n_blk):
        for s in range(steps):
            uperm_ref[s * rows:(s + 1) * rows, blk * LANES:(blk + 1) * LANES] = (
                unat_ref[blk, pl.ds(s, rows, stride=steps), :])

    def b_proj(blk):
        bu_ref[blk % 2] = _dot(uperm_ref[:, blk * LANES:(blk + 1) * LANES].astype(BF16),
                               bmat_ref[blk])

    def s5_block(blk):
        lanes = slice(blk * LANES, (blk + 1) * LANES)
        slot = blk % 2
        a_re = lam_ref[blk, 0:1, :]
        a_im = lam_ref[blk, 1:2, :]

        def advance(s, state, store):
            s_re, s_im = state
            b = bu_ref[slot, s * rows:(s + 1) * rows, :]
            n_re = a_re * s_re - a_im * s_im + b[:, :half]
            n_im = a_re * s_im + a_im * s_re + b[:, half:]
            if store:
                xs_ref[slot, s * rows:(s + 1) * rows, :] = jnp.concatenate([n_re, n_im], axis=1)
            return n_re, n_im

        state = (jnp.zeros((rows, half), F32), jnp.zeros((rows, half), F32))
        for s in range(steps):
            state = advance(s, state, False)
        e_ref[...] = jnp.concatenate(state, axis=1)
        ac_re = lamc_ref[blk, 0:1, :]
        ac_im = lamc_ref[blk, 1:2, :]
        c_re = s5carry_ref[blk, 0:1, :]
        c_im = s5carry_ref[blk, 1:2, :]
        for i in range(rows):
            cin_ref[i:i + 1, :] = jnp.concatenate([c_re, c_im], axis=1)
            e_row = e_ref[i:i + 1, :]
            c_re, c_im = (ac_re * c_re - ac_im * c_im + e_row[:, :half],
                          ac_re * c_im + ac_im * c_re + e_row[:, half:])
        s5carry_ref[blk, 0:1, :] = c_re
        s5carry_ref[blk, 1:2, :] = c_im
        cin = cin_ref[...]
        state = (cin[:, :half], cin[:, half:])
        for s in range(steps):
            state = advance(s, state, True)
        yperm_ref[:, lanes] = _dot(xs_ref[slot].astype(BF16), cmat_ref[blk])

    for blk in range(min(2, n_blk)):
        b_proj(blk)

    o0 = sw
    r0 = o0 + 2 * kw + vw
    q = _dot(h, wmain_ref[:, o0:o0 + kw]) * (dk ** -0.5)
    k = _dot(h, wmain_ref[:, o0 + kw:o0 + 2 * kw])
    v = _dot(h, wmain_ref[:, o0 + 2 * kw:o0 + 2 * kw + vw]).astype(BF16)
    a_lr = _dot(h, wa_ref[...])
    z = _dot(a_lr.astype(BF16), wup_ref[...]) + balpha_ref[...]
    log_a = _log_sigmoid(z) * (1.0 / GLA_GATE_TEMP)

    sub = causalf_ref.shape[0]
    subs = [slice(i * sub, (i + 1) * sub) for i in range(tile // sub)]
    g = jnp.concatenate([_split_dot(causalb_ref[...], log_a[rs]) for rs in subs], axis=0)
    g_last = jnp.concatenate([_split_dot(inchunk_ref[...], log_a[rs]) for rs in subs], axis=0)
    q_dec = (q * jnp.exp(g)).astype(BF16)
    k_dec = (k * jnp.exp(-g)).astype(BF16)
    k_upd = (k * jnp.exp(g_last - g)).astype(BF16)
    decay = jnp.exp(g_last)
    keep = causalf_ref[...] > 0.5
    gate = jax.nn.silu(_dot(h, wmain_ref[:, r0:r0 + vw]))

    def gla_head(hd):
        ks = slice(hd * dk, (hd + 1) * dk)
        vs = slice(hd * dv, (hd + 1) * dv)
        qh, kh, kuh, vh = q_dec[:, ks], k_dec[:, ks], k_upd[:, ks], v[:, vs]
        o_intra = jnp.concatenate(
            [_dot(jnp.where(keep, _dot_nt(qh[rs], kh[rs]), 0.0).astype(BF16), vh[rs]) for rs in subs],
            axis=0)
        chunks = [slice(c * GLA_CHUNK, (c + 1) * GLA_CHUNK) for c in range(n_chunks)]
        d_st = [_dot_tn(vh[rs], kuh[rs]) for rs in chunks]
        st = glastate_ref[hd]
        states = []
        for c in range(n_chunks):
            states.append(st.astype(BF16))
            st = st * decay[c * GLA_CHUNK:c * GLA_CHUNK + 1, ks] + d_st[c]
        glastate_ref[hd] = st
        for c, rs in enumerate(chunks):
            ogla_ref[rs, vs] = o_intra[rs] + _dot_nt(qh[rs], states[c])
        oh = ogla_ref[:, vs]
        ms = jnp.mean(oh * oh, axis=-1, keepdims=True)
        anat_ref[:, sw + hd * dv:sw + (hd + 1) * dv] = (
            oh * lax.rsqrt(ms + NORM_EPS) * gnorm_ref[:, vs] * gate[:, vs]).astype(BF16)

    s5_block(0)
    for i in range(max(n_blk - 1, GLA_HEADS)):
        if i + 2 < n_blk:
            b_proj(i + 2)
        if i < GLA_HEADS:
            gla_head(i)
        if i + 1 < n_blk:
            s5_block(i + 1)

    y = yperm_ref[...] + dskip_ref[...] * uperm_ref[...]
    y = jax.nn.gelu(y, approximate=True)
    a_perm = y * jax.nn.sigmoid(_dot(y.astype(BF16), wglu_ref[...]) + bglu_ref[...])
    anat_ref[:, 0:sw] = _dot(unperm_ref[...], a_perm.astype(BF16)).astype(BF16)

    o_ref[...] = x + _rms(_dot(anat_ref[...], wout_ref[...]), lnpost_ref[...])


def _s5_discretize(lam_re, lam_im, b_re, b_im, log_dt, power):
    dt = jnp.exp(log_dt)[:, None]
    mag = jnp.exp(lam_re * dt)
    lb_re = mag * jnp.cos(lam_im * dt)
    lb_im = mag * jnp.sin(lam_im * dt)
    inv = 1.0 / (lam_re * lam_re + lam_im * lam_im)
    zr = ((lb_re - 1.0) * lam_re + lb_im * lam_im) * inv
    zi = (lb_im * lam_re - (lb_re - 1.0) * lam_im) * inv
    bb_re = zr[..., None] * b_re - zi[..., None] * b_im
    bb_im = zr[..., None] * b_im + zi[..., None] * b_re
    magp = jnp.exp(lam_re * dt * power)
    lp_re = magp * jnp.cos(lam_im * dt * power)
    lp_im = magp * jnp.sin(lam_im * dt * power)
    return lb_re, lb_im, lp_re, lp_im, bb_re, bb_im


def _even_layer(x, ln_pre, ln_post, w_in, lam_re, lam_im, b_re, b_im, c_re, c_im, d_skip, log_dt,
                w_glu, b_glu, w_alpha_up, b_alpha, gla_norm, w_out):
    bsz, seq, d = x.shape
    groups, state = lam_re.shape
    gs = b_re.shape[-1]
    sw = groups * gs
    rank, kw = w_alpha_up.shape
    vw = gla_norm.shape[0]
    tile = min(EVEN_TILE, seq)
    lg = S5_LANE_GROUPS
    n_blk = groups // lg
    half = lg * state

    lb_re, lb_im, lp_re, lp_im, bb_re, bb_im = _s5_discretize(
        lam_re.astype(F32), lam_im.astype(F32), b_re.astype(F32), b_im.astype(F32),
        log_dt.astype(F32), float(S5_STEPS))

    def lam_pack(re, im):
        out = jnp.zeros((n_blk, SUBLANES, half), F32)
        return out.at[:, 0].set(re.reshape(n_blk, half)).at[:, 1].set(im.reshape(n_blk, half))

    same_group = (jnp.arange(lg * gs)[:, None] // gs) == (jnp.arange(half)[None, :] // state)

    def b_pack(bb):
        rows = bb.transpose(0, 2, 1).reshape(n_blk, lg * gs, state)
        return jnp.where(same_group, jnp.tile(rows, (1, 1, lg)), 0.0)

    def c_pack(cc):
        rows = cc.transpose(0, 2, 1).reshape(n_blk, half, gs)
        return jnp.where(same_group.T, jnp.tile(rows, (1, 1, lg)), 0.0)

    bmat = jnp.concatenate([b_pack(bb_re), b_pack(bb_im)], axis=2).astype(BF16)
    cmat = jnp.concatenate([c_pack(c_re.astype(F32)), -c_pack(c_im.astype(F32))], axis=1).astype(BF16)
    main = sw + 2 * kw + 2 * vw
    wa = jnp.zeros((d, LANES), F32).at[:, :rank].set(w_in[:, main:main + rank]).astype(BF16)
    wup = jnp.zeros((LANES, kw), F32).at[:rank].set(w_alpha_up).astype(BF16)
    args = (ln_pre.reshape(1, d), ln_post.reshape(1, d), w_in[:, :main].astype(BF16), wa, wup,
            b_alpha.reshape(1, kw), gla_norm.reshape(1, vw),
            lam_pack(lb_re, lb_im), lam_pack(lp_re, lp_im), bmat, cmat,
            d_skip.reshape(1, sw), w_glu.astype(BF16), b_glu.reshape(1, sw), w_out.astype(BF16))
    rows = tile // S5_STEPS
    r_idx = jnp.arange(tile, dtype=jnp.int32)[:, None]
    c_idx = jnp.arange(tile, dtype=jnp.int32)[None, :]
    unperm = (c_idx == (r_idx % S5_STEPS) * rows + r_idx // S5_STEPS)
    sub = min(GLA_SUB, tile)
    r_idx, c_idx = r_idx[:sub], c_idx[:, :sub]
    in_chunk = (r_idx // GLA_CHUNK == c_idx // GLA_CHUNK)
    causal = (in_chunk & (c_idx <= r_idx)).astype(F32)
    args = args + (causal, causal.astype(BF16), in_chunk.astype(BF16), unperm.astype(BF16))
    scratch = [
        pltpu.VMEM((tile, d), BF16),
        pltpu.VMEM((n_blk, tile, LANES), F32),
        pltpu.VMEM((tile, sw), F32),
        pltpu.VMEM((2, tile, 2 * half), F32),
        pltpu.VMEM((2, tile, 2 * half), F32),
        pltpu.VMEM((tile, sw), F32),
        pltpu.VMEM((rows, 2 * half), F32),
        pltpu.VMEM((rows, 2 * half), F32),
        pltpu.VMEM((tile, sw + vw), BF16),
        pltpu.VMEM((n_blk, SUBLANES, half), F32),
        pltpu.VMEM((tile, vw), F32),
        pltpu.VMEM((GLA_HEADS, vw // GLA_HEADS, kw // GLA_HEADS), F32),
    ]
    return pl.pallas_call(
        _even_kernel,
        grid=(bsz, seq // tile),
        in_specs=[_tile_spec(tile, d)] + [_full_spec(a) for a in args],
        out_specs=_tile_spec(tile, d),
        out_shape=jax.ShapeDtypeStruct(x.shape, x.dtype),
        scratch_shapes=scratch,
        compiler_params=_compiler_params(),
        name="even_mixer",
    )(x, *args)


def kernel(x, ln_mix_pre, ln_mix_post, ln_ffn_pre, ln_ffn_post, ev_w_in, s5_lambda_re, s5_lambda_im, s5_b_re, s5_b_im, s5_c_re, s5_c_im, s5_d, s5_log_dt, s5_w_glu, s5_b_glu, gla_w_alpha_up, gla_b_alpha, gla_norm, ev_w_out, od_w_in, ml_conv_w, ml_conv_b, ml_w_q, ml_w_k, ml_w_v, ml_w_gate, ml_b_gate, ml_norm, ml_skip, od_w_out, ffn_w_up, ffn_conv_w, ffn_conv_b, ffn_w_down):
    depth = ln_mix_pre.shape[0]
    for layer in range(depth):
        if layer % 2 == 0:
            e = layer // 2
            x = _even_layer(x, ln_mix_pre[layer], ln_mix_post[layer], ev_w_in[e], s5_lambda_re[e],
                            s5_lambda_im[e], s5_b_re[e], s5_b_im[e], s5_c_re[e], s5_c_im[e], s5_d[e],
                            s5_log_dt[e], s5_w_glu[e], s5_b_glu[e], gla_w_alpha_up[e], gla_b_alpha[e],
                            gla_norm[e], ev_w_out[e])
        else:
            o = layer // 2
            x = _odd_layer(x, ln_mix_pre[layer], ln_mix_post[layer], od_w_in[o], ml_conv_w[o],
                           ml_conv_b[o], ml_w_q[o], ml_w_k[o], ml_w_v[o], ml_w_gate[o], ml_b_gate[o],
                           ml_norm[o], ml_skip[o], od_w_out[o])
        x = _ffn_layer(x, ln_ffn_pre[layer], ln_ffn_post[layer], ffn_w_up[layer], ffn_conv_w[layer],
                       ffn_conv_b[layer], ffn_w_down[layer])
    return x


def _odd_kernel(x_ref, lnpre_ref, lnpost_ref, win_ref, cw_ref, cb_ref, wqk_ref, wv_ref, wfqk_ref, wfv_ref,
                bg_ref,
                mnorm_ref, skip_ref, wout_ref,
                o_ref,
                h_ref, xc_ref, xcb_ref, xmb_ref, q_ref, k_ref, v_ref, tail_ref, c_ref, n_ref,
                m_ref):
    tile = x_ref.shape[0]
    inner = xc_ref.shape[1]
    heads = c_ref.shape[0]
    dh = inner // heads
    n_blocks = wqk_ref.shape[0]
    bw = wqk_ref.shape[1]
    taps = MLSTM_CONV_TAPS

    @pl.when(pl.program_id(1) == 0)
    def _():
        tail_ref[...] = jnp.zeros_like(tail_ref)
        c_ref[...] = jnp.zeros_like(c_ref)
        n_ref[...] = jnp.zeros_like(n_ref)
        m_ref[...] = jnp.zeros_like(m_ref)

    h_ref[...] = _rms(x_ref[...], lnpre_ref[...]).astype(BF16)

    def conv(xm, w):
        out = w[taps - 1:taps] * xm
        for kk in range(taps - 1):
            out = out + w[kk:kk + 1] * pltpu.roll(xm, taps - 1 - kk, axis=0)
        return out

    gw = 2 * bw
    n_groups = inner // gw

    def in_proj(grp):
        return _dot(h_ref[...], win_ref[:, grp * gw:(grp + 1) * gw])

    def conv_silu(grp, xm):
        ls = slice(grp * gw, (grp + 1) * gw)
        w = cw_ref[:, ls]
        b = cb_ref[:, ls]
        pre = conv(xm, w) + b
        head = jnp.concatenate([tail_ref[:, ls], xm[0:SUBLANES]], axis=0)
        pre_head = (conv(head, w) + b)[SUBLANES:2 * SUBLANES]
        tail_ref[:, ls] = xm[tile - SUBLANES:tile]
        xc = jax.nn.silu(jnp.concatenate([pre_head, pre[SUBLANES:]], axis=0))
        xc_ref[:, ls] = xc
        xcb_ref[:, ls] = xc.astype(BF16)
        xmb_ref[:, ls] = xm.astype(BF16)

    def headwise(grp):
        for blk in range(grp * (gw // bw), (grp + 1) * (gw // bw)):
            ls = slice(blk * bw, (blk + 1) * bw)
            qk = _dot(xcb_ref[:, ls], wqk_ref[blk])
            q_ref[:, ls] = qk[:, :bw].astype(BF16)
            k_ref[:, ls] = qk[:, bw:].astype(BF16)
            v_ref[:, ls] = _dot(xmb_ref[:, ls], wv_ref[blk]).astype(BF16)
        gs = slice(grp * gw, (grp + 1) * gw)
        gate_parts.append(_dot(xcb_ref[:, gs], wfqk_ref[gs, :]) + _dot(xmb_ref[:, gs], wfv_ref[gs, :]))

    gate_parts = []

    xm_next = in_proj(0)
    for grp in range(n_groups):
        xm = xm_next
        if grp + 1 < n_groups:
            xm_next = in_proj(grp + 1)
        if grp > 0:
            headwise(grp - 1)
        conv_silu(grp, xm)
    headwise(n_groups - 1)

    cs = min(ODD_CHUNK, tile)
    items = [(ck, hd) for ck in range(tile // cs) for hd in range(heads)]

    def state_matmuls(ck, hd):
        rs = slice(ck * cs, (ck + 1) * cs)
        hs = slice(hd * dh, (hd + 1) * dh)
        qh = q_ref[rs, hs]
        c_mat = c_ref[hd]
        return _dot_nt(qh, k_ref[rs, hs]), _dot(qh, c_mat.astype(BF16)), c_mat

    ahead = state_matmuls(*items[0])

    gates_all = bg_ref[...] + functools.reduce(jnp.add, gate_parts)
    log_f = _log_sigmoid(gates_all)
    row = lax.broadcasted_iota(jnp.int32, (cs, cs), 0)
    col = lax.broadcasted_iota(jnp.int32, (cs, cs), 1)
    causal = col <= row
    tri = jnp.where(causal, 1.0, 0.0).astype(BF16)
    chunk_gates = []
    for ck in range(tile // cs):
        rs = slice(ck * cs, (ck + 1) * cs)
        g_ck = gates_all[rs]
        f_ck = _split_dot(tri, log_f[rs])
        chunk_gates.append((g_ck, f_ck, jnp.transpose(g_ck), jnp.transpose(f_ck)))
    scale = dh ** -0.5

    for idx, (ck, hd) in enumerate(items):
        qk_raw, inter_raw, c_mat = ahead
        if idx + 1 < len(items):
            ahead = state_matmuls(*items[idx + 1])
        rs = slice(ck * cs, (ck + 1) * cs)
        hs = slice(hd * dh, (hd + 1) * dh)
        gates, f_cum, gates_t, f_cum_t = chunk_gates[ck]
        f_col = f_cum[:, heads + hd:heads + hd + 1]
        f_row = f_cum_t[heads + hd:heads + hd + 1, :]
        i_col = gates[:, hd:hd + 1]
        i_row = gates_t[hd:hd + 1, :]
        m_prev = m_ref[hd:hd + 1, 0:1]
        d_log = jnp.where(causal, f_col - f_row + i_row, -jnp.inf)
        inter_log = f_col + m_prev
        m_loc = jnp.maximum(inter_log, jnp.max(d_log, axis=-1, keepdims=True))
        qh = q_ref[rs, hs]
        kh = k_ref[rs, hs]
        vh = v_ref[rs, hs]
        s = qk_raw * (scale * jnp.exp(d_log - m_loc))
        w_inter = jnp.exp(inter_log - m_loc)
        num = _dot(s.astype(BF16), vh) + w_inter * inter_raw
        o_pre = _dot(h_ref[rs, :], win_ref[:, inner + hd * dh:inner + (hd + 1) * dh])
        n_vec = n_ref[hd:hd + 1, :]
        den = (jnp.sum(s, axis=-1, keepdims=True)
               + w_inter * jnp.sum(qh.astype(F32) * n_vec, axis=-1, keepdims=True))
        denom = jnp.maximum(jnp.abs(den), jnp.exp(-m_loc))
        hh = num * (1.0 / denom)
        ms = jnp.mean(hh * hh, axis=-1, keepdims=True)
        hn = hh * lax.rsqrt(ms + NORM_EPS) * mnorm_ref[:, hs] + skip_ref[:, hs] * xc_ref[rs, hs]
        gated = (jax.nn.sigmoid(o_pre) * hn).astype(BF16)
        part = _dot(gated, wout_ref[hs, :])
        mix = part if hd == 0 else mix + part
        f_last = f_col[cs - 1:cs, :]
        w_log = f_last - f_col + i_col
        m_new = jnp.maximum(f_last + m_prev, jnp.max(w_log, axis=0, keepdims=True))
        w_upd = jnp.exp(w_log - m_new)
        decay = jnp.exp(f_last + m_prev - m_new)
        kw = kh.astype(F32) * (scale * w_upd)
        c_ref[hd] = decay * c_mat + _dot_tn(kw.astype(BF16), vh)
        n_ref[hd:hd + 1, :] = decay * n_vec + jnp.sum(kw, axis=0, keepdims=True)
        m_ref[hd:hd + 1, :] = jnp.broadcast_to(m_new, (1, m_ref.shape[1]))
        if hd == heads - 1:
            o_ref[rs, :] = x_ref[rs, :] + _rms(mix, lnpost_ref[...])


def _odd_layer(x, ln_pre, ln_post, w_in, conv_w, conv_b, w_q, w_k, w_v, w_gate, b_gate, m_norm, skip,
               w_out):
    bsz, seq, d = x.shape
    inner = w_out.shape[0]
    heads = b_gate.shape[0] // 2
    tile = min(ODD_TILE, seq)
    bw = MXU_DIM
    n_blocks = inner // bw
    per = bw // MLSTM_QKV_BLOCK
    on_diag = (jnp.arange(bw)[:, None] // MLSTM_QKV_BLOCK) == (jnp.arange(bw)[None, :] // MLSTM_QKV_BLOCK)

    def blockdiag(w):
        rows = w.astype(F32).reshape(n_blocks, bw, MLSTM_QKV_BLOCK)
        return jnp.where(on_diag, jnp.tile(rows, (1, 1, per)), 0.0)

    wqk = jnp.concatenate([blockdiag(w_q), blockdiag(w_k)], axis=2).astype(BF16)
    wv = blockdiag(w_v).astype(BF16)
    nb = inner // MLSTM_QKV_BLOCK
    wg_blocks = w_gate.astype(F32).reshape(3, nb, MLSTM_QKV_BLOCK, 2 * heads)

    def fold(w, g):
        return jnp.einsum('ncd,ndg->ncg', w.astype(F32), g,
                          precision=lax.Precision.HIGHEST).reshape(inner, 2 * heads)

    def pad_lanes(w):
        return jnp.zeros((inner, LANES), F32).at[:, :2 * heads].set(w).astype(BF16)

    wfqk = pad_lanes(fold(w_q, wg_blocks[0]) + fold(w_k, wg_blocks[1]))
    wfv = pad_lanes(fold(w_v, wg_blocks[2]))
    bg = jnp.zeros((1, LANES), F32).at[0, :2 * heads].set(b_gate)
    cw = jnp.zeros((SUBLANES, inner), F32).at[:conv_w.shape[0]].set(conv_w)
    args = (ln_pre.reshape(1, d), ln_post.reshape(1, d), w_in.astype(BF16), cw, conv_b.reshape(1, inner),
            wqk, wv, wfqk, wfv, bg, m_norm.reshape(1, inner), skip.reshape(1, inner), w_out.astype(BF16))
    dh = inner // heads
    scratch = [
        pltpu.VMEM((tile, d), BF16),
        pltpu.VMEM((tile, inner), F32),
        pltpu.VMEM((tile, inner), BF16),
        pltpu.VMEM((tile, inner), BF16),
        pltpu.VMEM((tile, inner), BF16),
        pltpu.VMEM((tile, inner), BF16),
        pltpu.VMEM((tile, inner), BF16),
        pltpu.VMEM((SUBLANES, inner), F32),
        pltpu.VMEM((heads, dh, dh), F32),
        pltpu.VMEM((SUBLANES, dh), F32),
        pltpu.VMEM((SUBLANES, LANES), F32),
    ]
    return pl.pallas_call(
        _odd_kernel,
        grid=(bsz, seq // tile),
        in_specs=[_tile_spec(tile, d)] + [_full_spec(a) for a in args],
        out_specs=_tile_spec(tile, d),
        out_shape=jax.ShapeDtypeStruct(x.shape, x.dtype),
        scratch_shapes=scratch,
        compiler_params=_compiler_params(),
        name="odd_mixer",
    )(x, *args)
```

```python
import functools

import jax
import jax.numpy as jnp
from jax import lax
from jax.experimental import pallas as pl
from jax.experimental.pallas import tpu as pltpu

F32 = jnp.float32
BF16 = jnp.bfloat16

NORM_EPS = 1e-6

S5_GROUP_SIZE = 16
GLA_HEADS = 4
GLA_CHUNK = 64
GLA_GATE_TEMP = 16.0
MLSTM_QKV_BLOCK = 4
MLSTM_CONV_TAPS = 4

LANES = 128
SUBLANES = 8
MXU_DIM = 256
VMEM_LIMIT_BYTES = 58 * 1024 * 1024

EVEN_TILE = 512
S5_STEPS = 32
ODD_TILE = 512
ODD_CHUNK = 256
FFN_TILE = 1024
FFN_COLS = 256
S5_LANE_GROUPS = LANES // S5_GROUP_SIZE
GLA_SUB = EVEN_TILE

def _dot(a, b):
    return jnp.dot(a, b, preferred_element_type=F32)


def _dot_nt(a, b):
    return lax.dot_general(a, b, (((1,), (1,)), ((), ())), preferred_element_type=F32)


def _dot_tn(a, b):
    return lax.dot_general(a, b, (((0,), (0,)), ((), ())), preferred_element_type=F32)


def _rms(x, g):
    ms = jnp.mean(x * x, axis=-1, keepdims=True)
    return x * lax.rsqrt(ms + NORM_EPS) * g


def _log_sigmoid(x):
    return jnp.minimum(x, 0.0) - jnp.log1p(jnp.exp(-jnp.abs(x)))


def _split_dot(mask_bf16, x):
    hi = x.astype(BF16)
    lo = (x - hi.astype(F32)).astype(BF16)
    return _dot(mask_bf16, hi) + _dot(mask_bf16, lo)


def _full_spec(arr):
    nd = arr.ndim
    return pl.BlockSpec(arr.shape, lambda b, l, _nd=nd: (0,) * _nd,
                        pipeline_mode=pl.Buffered(1))


def _tile_spec(tile, d):
    return pl.BlockSpec((None, tile, d), lambda b, l: (b, l, 0))


def _compiler_params():
    return pltpu.CompilerParams(dimension_semantics=("arbitrary", "arbitrary"),
                                vmem_limit_bytes=VMEM_LIMIT_BYTES)


def _ffn_kernel(x_ref, lnpre_ref, lnpost_ref, wup_ref, cw_ref, cb_ref, wd_ref,
                o_ref, h_ref, act_ref, tail_ref):
    tile = x_ref.shape[0]
    f = wd_ref.shape[0]
    n_chunks = tail_ref.shape[0]
    fc = tail_ref.shape[2]

    @pl.when(pl.program_id(1) == 0)
    def _():
        tail_ref[...] = jnp.zeros_like(tail_ref)

    h_ref[...] = _rms(x_ref[...], lnpre_ref[...]).astype(BF16)

    def conv3(g, w):
        return (w[0:1] * pltpu.roll(g, 2, axis=0) + w[1:2] * pltpu.roll(g, 1, axis=0)
                + w[2:3] * g)

    def up(c):
        h = h_ref[...]
        return (_dot(h, wup_ref[:, c * fc:(c + 1) * fc]),
                _dot(h, wup_ref[:, f + c * fc:f + (c + 1) * fc]))

    def gate(c, g, u):
        w = cw_ref[:, c * fc:(c + 1) * fc]
        b = cb_ref[:, c * fc:(c + 1) * fc]
        conv = conv3(g, w) + b
        head = jnp.concatenate([tail_ref[c], g[0:SUBLANES]], axis=0)
        conv_head = (conv3(head, w) + b)[SUBLANES:2 * SUBLANES]
        tail_ref[c] = g[tile - SUBLANES:tile]
        conv = jnp.concatenate([conv_head, conv[SUBLANES:]], axis=0)
        act_ref[:, c * fc:(c + 1) * fc] = (jax.nn.gelu(conv, approximate=True) * u).astype(BF16)

    gu = up(0)
    for c in range(n_chunks):
        gu_next = up(c + 1) if c + 1 < n_chunks else None
        gate(c, *gu)
        gu = gu_next
    o_ref[...] = x_ref[...] + _rms(_dot(act_ref[...], wd_ref[...]), lnpost_ref[...])


def _ffn_layer(x, ln_pre, ln_post, w_up, conv_w, conv_b, w_down):
    bsz, seq, d = x.shape
    f = w_down.shape[0]
    tile = min(FFN_TILE, seq)
    nc = f // FFN_COLS
    taps = conv_w.shape[0]
    cw = jnp.zeros((SUBLANES, f), F32).at[:taps].set(conv_w)
    args = (ln_pre.reshape(1, d), ln_post.reshape(1, d), w_up.astype(BF16), cw, conv_b.reshape(1, f),
            w_down.astype(BF16))
    return pl.pallas_call(
        _ffn_kernel,
        grid=(bsz, seq // tile),
        in_specs=[_tile_spec(tile, d)] + [_full_spec(a) for a in args],
        out_specs=_tile_spec(tile, d),
        out_shape=jax.ShapeDtypeStruct(x.shape, x.dtype),
        scratch_shapes=[pltpu.VMEM((tile, d), BF16), pltpu.VMEM((tile, f), BF16),
                        pltpu.VMEM((nc, SUBLANES, FFN_COLS), F32)],
        compiler_params=_compiler_params(),
        name="conv_ffn",
    )(x, *args)


def _even_kernel(x_ref, lnpre_ref, lnpost_ref, wmain_ref, wa_ref, wup_ref, balpha_ref, gnorm_ref,
                 lam_ref, lamc_ref, bmat_ref, cmat_ref, dskip_ref, wglu_ref, bglu_ref, wout_ref,
                 causalf_ref, causalb_ref, inchunk_ref, unperm_ref,
                 o_ref,
                 h_ref, unat_ref, uperm_ref, bu_ref, xs_ref, yperm_ref, e_ref, cin_ref, anat_ref,
                 s5carry_ref, ogla_ref, glastate_ref):
    tile = x_ref.shape[0]
    n_blk = unat_ref.shape[0]
    sw = n_blk * LANES
    half = bu_ref.shape[2] // 2
    steps = S5_STEPS
    rows = tile // steps
    kw = wup_ref.shape[1]
    vw = ogla_ref.shape[1]
    dk = kw // GLA_HEADS
    dv = vw // GLA_HEADS
    n_chunks = tile // GLA_CHUNK

    @pl.when(pl.program_id(1) == 0)
    def _():
        s5carry_ref[...] = jnp.zeros_like(s5carry_ref)
        glastate_ref[...] = jnp.zeros_like(glastate_ref)

    h_ref[...] = _rms(x_ref[...], lnpre_ref[...]).astype(BF16)
    h = h_ref[...]

    u = _dot(h, wmain_ref[:, 0:sw])
    for blk in range(n_blk):
        unat_ref[blk] = u[:, blk * LANES:(blk + 1) * LANES]
    for blk in range(n_blk):
        for s in range(steps):
            uperm_ref[s * rows:(s + 1) * rows, blk * LANES:(blk + 1) * LANES] = (
                unat_ref[blk, pl.ds(s, rows, stride=steps), :])

    def b_proj(blk):
        bu_ref[blk % 2] = _dot(uperm_ref[:, blk * LANES:(blk + 1) * LANES].astype(BF16),
                               bmat_ref[blk])

    def s5_block(blk):
        lanes = slice(blk * LANES, (blk + 1) * LANES)
        slot = blk % 2
        a_re = lam_ref[blk, 0:1, :]
        a_im = lam_ref[blk, 1:2, :]

        def advance(s, state, store):
            s_re, s_im = state
            b = bu_ref[slot, s * rows:(s + 1) * rows, :]
            n_re = a_re * s_re - a_im * s_im + b[:, :half]
            n_im = a_re * s_im + a_im * s_re + b[:, half:]
            if store:
                xs_ref[slot, s * rows:(s + 1) * rows, :] = jnp.concatenate([n_re, n_im], axis=1)
            return n_re, n_im

        state = (jnp.zeros((rows, half), F32), jnp.zeros((rows, half), F32))
        for s in range(steps):
            state = advance(s, state, False)
        e_ref[...] = jnp.concatenate(state, axis=1)
        ac_re = lamc_ref[blk, 0:1, :]
        ac_im = lamc_ref[blk, 1:2, :]
        c_re = s5carry_ref[blk, 0:1, :]
        c_im = s5carry_ref[blk, 1:2, :]
        for i in range(rows):
            cin_ref[i:i + 1, :] = jnp.concatenate([c_re, c_im], axis=1)
            e_row = e_ref[i:i + 1, :]
            c_re, c_im = (ac_re * c_re - ac_im * c_im + e_row[:, :half],
                          ac_re * c_im + ac_im * c_re + e_row[:, half:])
        s5carry_ref[blk, 0:1, :] = c_re
        s5carry_ref[blk, 1:2, :] = c_im
        cin = cin_ref[...]
        state = (cin[:, :half], cin[:, half:])
        for s in range(steps):
            state = advance(s, state, True)
        yperm_ref[:, lanes] = _dot(xs_ref[slot].astype(BF16), cmat_ref[blk])

    for blk in range(min(2, n_blk)):
        b_proj(blk)

    o0 = sw
    r0 = o0 + 2 * kw + vw
    q = _dot(h, wmain_ref[:, o0:o0 + kw]) * (dk ** -0.5)
    k = _dot(h, wmain_ref[:, o0 + kw:o0 + 2 * kw])
    v = _dot(h, wmain_ref[:, o0 + 2 * kw:o0 + 2 * kw + vw]).astype(BF16)
    a_lr = _dot(h, wa_ref[...])
    z = _dot(a_lr.astype(BF16), wup_ref[...]) + balpha_ref[...]
    log_a = _log_sigmoid(z) * (1.0 / GLA_GATE_TEMP)

    sub = causalf_ref.shape[0]
    subs = [slice(i * sub, (i + 1) * sub) for i in range(tile // sub)]
    g = jnp.concatenate([_split_dot(causalb_ref[...], log_a[rs]) for rs in subs], axis=0)
    g_last = jnp.concatenate([_split_dot(inchunk_ref[...], log_a[rs]) for rs in subs], axis=0)
    q_dec = (q * jnp.exp(g)).astype(BF16)
    k_dec = (k * jnp.exp(-g)).astype(BF16)
    k_upd = (k * jnp.exp(g_last - g)).astype(BF16)
    decay = jnp.exp(g_last)
    keep = causalf_ref[...] > 0.5
    gate = jax.nn.silu(_dot(h, wmain_ref[:, r0:r0 + vw]))

    def gla_head(hd):
        ks = slice(hd * dk, (hd + 1) * dk)
        vs = slice(hd * dv, (hd + 1) * dv)
        qh, kh, kuh, vh = q_dec[:, ks], k_dec[:, ks], k_upd[:, ks], v[:, vs]
        o_intra = jnp.concatenate(
            [_dot(jnp.where(keep, _dot_nt(qh[rs], kh[rs]), 0.0).astype(BF16), vh[rs]) for rs in subs],
            axis=0)
        chunks = [slice(c * GLA_CHUNK, (c + 1) * GLA_CHUNK) for c in range(n_chunks)]
        d_st = [_dot_tn(vh[rs], kuh[rs]) for rs in chunks]
        st = glastate_ref[hd]
        states = []
        for c in range(n_chunks):
            states.append(st.astype(BF16))
            st = st * decay[c * GLA_CHUNK:c * GLA_CHUNK + 1, ks] + d_st[c]
        glastate_ref[hd] = st
        for c, rs in enumerate(chunks):
            ogla_ref[rs, vs] = o_intra[rs] + _dot_nt(qh[rs], states[c])
        oh = ogla_ref[:, vs]
        ms = jnp.mean(oh * oh, axis=-1, keepdims=True)
        anat_ref[:, sw + hd * dv:sw + (hd + 1) * dv] = (
            oh * lax.rsqrt(ms + NORM_EPS) * gnorm_ref[:, vs] * gate[:, vs]).astype(BF16)

    s5_block(0)
    for i in range(max(n_blk - 1, GLA_HEADS)):
        if i + 2 < n_blk:
            b_proj(i + 2)
        if i < GLA_HEADS:
            gla_head(i)
        if i + 1 < n_blk:
            s5_block(i + 1)

    y = yperm_ref[...] + dskip_ref[...] * uperm_ref[...]
    y = jax.nn.gelu(y, approximate=True)
    a_perm = y * jax.nn.sigmoid(_dot(y.astype(BF16), wglu_ref[...]) + bglu_ref[...])
    anat_ref[:, 0:sw] = _dot(unperm_ref[...], a_perm.astype(BF16)).astype(BF16)

    o_ref[...] = x_ref[...] + _rms(_dot(anat_ref[...], wout_ref[...]), lnpost_ref[...])


def _s5_discretize(lam_re, lam_im, b_re, b_im, log_dt, power):
    dt = jnp.exp(log_dt)[:, None]
    mag = jnp.exp(lam_re * dt)
    lb_re = mag * jnp.cos(lam_im * dt)
    lb_im = mag * jnp.sin(lam_im * dt)
    inv = 1.0 / (lam_re * lam_re + lam_im * lam_im)
    zr = ((lb_re - 1.0) * lam_re + lb_im * lam_im) * inv
    zi = (lb_im * lam_re - (lb_re - 1.0) * lam_im) * inv
    bb_re = zr[..., None] * b_re - zi[..., None] * b_im
    bb_im = zr[..., None] * b_im + zi[..., None] * b_re
    magp = jnp.exp(lam_re * dt * power)
    lp_re = magp * jnp.cos(lam_im * dt * power)
    lp_im = magp * jnp.sin(lam_im * dt * power)
    return lb_re, lb_im, lp_re, lp_im, bb_re, bb_im


def _even_layer(x, ln_pre, ln_post, w_in, lam_re, lam_im, b_re, b_im, c_re, c_im, d_skip, log_dt,
                w_glu, b_glu, w_alpha_up, b_alpha, gla_norm, w_out):
    bsz, seq, d = x.shape
    groups, state = lam_re.shape
    gs = b_re.shape[-1]
    sw = groups * gs
    rank, kw = w_alpha_up.shape
    vw = gla_norm.shape[0]
    tile = min(EVEN_TILE, seq)
    lg = S5_LANE_GROUPS
    n_blk = groups // lg
    half = lg * state

    lb_re, lb_im, lp_re, lp_im, bb_re, bb_im = _s5_discretize(
        lam_re.astype(F32), lam_im.astype(F32), b_re.astype(F32), b_im.astype(F32),
        log_dt.astype(F32), float(S5_STEPS))

    def lam_pack(re, im):
        out = jnp.zeros((n_blk, SUBLANES, half), F32)
        return out.at[:, 0].set(re.reshape(n_blk, half)).at[:, 1].set(im.reshape(n_blk, half))

    same_group = (jnp.arange(lg * gs)[:, None] // gs) == (jnp.arange(half)[None, :] // state)

    def b_pack(bb):
        rows = bb.transpose(0, 2, 1).reshape(n_blk, lg * gs, state)
        return jnp.where(same_group, jnp.tile(rows, (1, 1, lg)), 0.0)

    def c_pack(cc):
        rows = cc.transpose(0, 2, 1).reshape(n_blk, half, gs)
        return jnp.where(same_group.T, jnp.tile(rows, (1, 1, lg)), 0.0)

    bmat = jnp.concatenate([b_pack(bb_re), b_pack(bb_im)], axis=2).astype(BF16)
    cmat = jnp.concatenate([c_pack(c_re.astype(F32)), -c_pack(c_im.astype(F32))], axis=1).astype(BF16)
    main = sw + 2 * kw + 2 * vw
    wa = jnp.zeros((d, LANES), F32).at[:, :rank].set(w_in[:, main:main + rank]).astype(BF16)
    wup = jnp.zeros((LANES, kw), F32).at[:rank].set(w_alpha_up).astype(BF16)
    args = (ln_pre.reshape(1, d), ln_post.reshape(1, d), w_in[:, :main].astype(BF16), wa, wup,
            b_alpha.reshape(1, kw), gla_norm.reshape(1, vw),
            lam_pack(lb_re, lb_im), lam_pack(lp_re, lp_im), bmat, cmat,
            d_skip.reshape(1, sw), w_glu.astype(BF16), b_glu.reshape(1, sw), w_out.astype(BF16))
    rows = tile // S5_STEPS
    r_idx = jnp.arange(tile, dtype=jnp.int32)[:, None]
    c_idx = jnp.arange(tile, dtype=jnp.int32)[None, :]
    unperm = (c_idx == (r_idx % S5_STEPS) * rows + r_idx // S5_STEPS)
    sub = min(GLA_SUB, tile)
    r_idx, c_idx = r_idx[:sub], c_idx[:, :sub]
    in_chunk = (r_idx // GLA_CHUNK == c_idx // GLA_CHUNK)
    causal = (in_chunk & (c_idx <= r_idx)).astype(F32)
    args = args + (causal, causal.astype(BF16), in_chunk.astype(BF16), unperm.astype(BF16))
    scratch = [
        pltpu.VMEM((tile, d), BF16),
        pltpu.VMEM((n_blk, tile, LANES), F32),
        pltpu.VMEM((tile, sw), F32),
        pltpu.VMEM((2, tile, 2 * half), F32),
        pltpu.VMEM((2, tile, 2 * half), F32),
        pltpu.VMEM((tile, sw), F32),
        pltpu.VMEM((rows, 2 * half), F32),
        pltpu.VMEM((rows, 2 * half), F32),
        pltpu.VMEM((tile, sw + vw), BF16),
        pltpu.VMEM((n_blk, SUBLANES, half), F32),
        pltpu.VMEM((tile, vw), F32),
        pltpu.VMEM((GLA_HEADS, vw // GLA_HEADS, kw // GLA_HEADS), F32),
    ]
    return pl.pallas_call(
        _even_kernel,
        grid=(bsz, seq // tile),
        in_specs=[_tile_spec(tile, d)] + [_full_spec(a) for a in args],
        out_specs=_tile_spec(tile, d),
        out_shape=jax.ShapeDtypeStruct(x.shape, x.dtype),
        scratch_shapes=scratch,
        compiler_params=_compiler_params(),
        name="even_mixer",
    )(x, *args)


def kernel(x, ln_mix_pre, ln_mix_post, ln_ffn_pre, ln_ffn_post, ev_w_in, s5_lambda_re, s5_lambda_im, s5_b_re, s5_b_im, s5_c_re, s5_c_im, s5_d, s5_log_dt, s5_w_glu, s5_b_glu, gla_w_alpha_up, gla_b_alpha, gla_norm, ev_w_out, od_w_in, ml_conv_w, ml_conv_b, ml_w_q, ml_w_k, ml_w_v, ml_w_gate, ml_b_gate, ml_norm, ml_skip, od_w_out, ffn_w_up, ffn_conv_w, ffn_conv_b, ffn_w_down):
    depth = ln_mix_pre.shape[0]
    for layer in range(depth):
        if layer % 2 == 0:
            e = layer // 2
            x = _even_layer(x, ln_mix_pre[layer], ln_mix_post[layer], ev_w_in[e], s5_lambda_re[e],
                            s5_lambda_im[e], s5_b_re[e], s5_b_im[e], s5_c_re[e], s5_c_im[e], s5_d[e],
                            s5_log_dt[e], s5_w_glu[e], s5_b_glu[e], gla_w_alpha_up[e], gla_b_alpha[e],
                            gla_norm[e], ev_w_out[e])
        else:
            o = layer // 2
            x = _odd_layer(x, ln_mix_pre[layer], ln_mix_post[layer], od_w_in[o], ml_conv_w[o],
                           ml_conv_b[o], ml_w_q[o], ml_w_k[o], ml_w_v[o], ml_w_gate[o], ml_b_gate[o],
                           ml_norm[o], ml_skip[o], od_w_out[o])
        x = _ffn_layer(x, ln_ffn_pre[layer], ln_ffn_post[layer], ffn_w_up[layer], ffn_conv_w[layer],
                       ffn_conv_b[layer], ffn_w_down[layer])
    return x


def _odd_kernel(x_ref, lnpre_ref, lnpost_ref, win_ref, cw_ref, cb_ref, wqk_ref, wv_ref, wfqk_ref, wfv_ref,
                bg_ref,
                mnorm_ref, skip_ref, wout_ref,
                o_ref,
                h_ref, xc_ref, xcb_ref, xmb_ref, q_ref, k_ref, v_ref, tail_ref, c_ref, n_ref,
                m_ref, pkw_ref, pv_ref, pdecay_ref):
    tile = x_ref.shape[0]
    inner = xc_ref.shape[1]
    heads = c_ref.shape[0]
    dh = inner // heads
    n_blocks = wqk_ref.shape[0]
    bw = wqk_ref.shape[1]
    taps = MLSTM_CONV_TAPS

    @pl.when(pl.program_id(1) == 0)
    def _():
        tail_ref[...] = jnp.zeros_like(tail_ref)
        c_ref[...] = jnp.zeros_like(c_ref)
        n_ref[...] = jnp.zeros_like(n_ref)
        m_ref[...] = jnp.zeros_like(m_ref)
        pkw_ref[...] = jnp.zeros_like(pkw_ref)
        pv_ref[...] = jnp.zeros_like(pv_ref)
        pdecay_ref[...] = jnp.zeros_like(pdecay_ref)

    def apply_pending(hd):
        hs = slice(hd * dh, (hd + 1) * dh)
        c_ref[hd] = (pdecay_ref[hd:hd + 1, 0:1] * c_ref[hd]
                     + _dot_tn(pkw_ref[:, hs], pv_ref[:, hs]))

    for hd in range(heads):
        apply_pending(hd)

    h_ref[...] = _rms(x_ref[...], lnpre_ref[...]).astype(BF16)

    def conv(xm, w):
        out = w[taps - 1:taps] * xm
        for kk in range(taps - 1):
            out = out + w[kk:kk + 1] * pltpu.roll(xm, taps - 1 - kk, axis=0)
        return out

    gw = 2 * bw
    n_groups = inner // gw

    def in_proj(grp):
        return _dot(h_ref[...], win_ref[:, grp * gw:(grp + 1) * gw])

    def conv_silu(grp, xm):
        ls = slice(grp * gw, (grp + 1) * gw)
        w = cw_ref[:, ls]
        b = cb_ref[:, ls]
        pre = conv(xm, w) + b
        head = jnp.concatenate([tail_ref[:, ls], xm[0:SUBLANES]], axis=0)
        pre_head = (conv(head, w) + b)[SUBLANES:2 * SUBLANES]
        tail_ref[:, ls] = xm[tile - SUBLANES:tile]
        xc = jax.nn.silu(jnp.concatenate([pre_head, pre[SUBLANES:]], axis=0))
        xc_ref[:, ls] = xc
        xcb_ref[:, ls] = xc.astype(BF16)
        xmb_ref[:, ls] = xm.astype(BF16)

    def headwise(grp):
        for blk in range(grp * (gw // bw), (grp + 1) * (gw // bw)):
            ls = slice(blk * bw, (blk + 1) * bw)
            qk = _dot(xcb_ref[:, ls], wqk_ref[blk])
            q_ref[:, ls] = qk[:, :bw].astype(BF16)
            k_ref[:, ls] = qk[:, bw:].astype(BF16)
            v_ref[:, ls] = _dot(xmb_ref[:, ls], wv_ref[blk]).astype(BF16)
        gs = slice(grp * gw, (grp + 1) * gw)
        gate_parts.append(_dot(xcb_ref[:, gs], wfqk_ref[gs, :]) + _dot(xmb_ref[:, gs], wfv_ref[gs, :]))

    gate_parts = []

    xm_next = in_proj(0)
    for grp in range(n_groups):
        xm = xm_next
        if grp + 1 < n_groups:
            xm_next = in_proj(grp + 1)
        if grp > 0:
            headwise(grp - 1)
        conv_silu(grp, xm)
    headwise(n_groups - 1)

    cs = min(ODD_CHUNK, tile)
    items = [(ck, hd) for ck in range(tile // cs) for hd in range(heads)]

    def state_matmuls(ck, hd):
        rs = slice(ck * cs, (ck + 1) * cs)
        hs = slice(hd * dh, (hd + 1) * dh)
        qh = q_ref[rs, hs]
        if ck > 0:
            apply_pending(hd)
        return _dot_nt(qh, k_ref[rs, hs]), _dot(qh, c_ref[hd].astype(BF16))

    ahead = state_matmuls(*items[0])

    gates_all = bg_ref[...] + functools.reduce(jnp.add, gate_parts)
    log_f = _log_sigmoid(gates_all)
    row = lax.broadcasted_iota(jnp.int32, (cs, cs), 0)
    col = lax.broadcasted_iota(jnp.int32, (cs, cs), 1)
    causal = col <= row
    tri = jnp.where(causal, 1.0, 0.0).astype(BF16)
    chunk_gates = []
    for ck in range(tile // cs):
        rs = slice(ck * cs, (ck + 1) * cs)
        g_ck = gates_all[rs]
        f_ck = _split_dot(tri, log_f[rs])
        chunk_gates.append((g_ck, f_ck, jnp.transpose(g_ck), jnp.transpose(f_ck)))
    scale = dh ** -0.5

    for idx, (ck, hd) in enumerate(items):
        qk_raw, inter_raw = ahead
        if idx + 1 < len(items):
            ahead = state_matmuls(*items[idx + 1])
        rs = slice(ck * cs, (ck + 1) * cs)
        hs = slice(hd * dh, (hd + 1) * dh)
        gates, f_cum, gates_t, f_cum_t = chunk_gates[ck]
        f_col = f_cum[:, heads + hd:heads + hd + 1]
        f_row = f_cum_t[heads + hd:heads + hd + 1, :]
        i_col = gates[:, hd:hd + 1]
        i_row = gates_t[hd:hd + 1, :]
        m_prev = m_ref[hd:hd + 1, 0:1]
        d_log = jnp.where(causal, f_col - f_row + i_row, -jnp.inf)
        inter_log = f_col + m_prev
        m_loc = jnp.maximum(inter_log, jnp.max(d_log, axis=-1, keepdims=True))
        qh = q_ref[rs, hs]
        kh = k_ref[rs, hs]
        vh = v_ref[rs, hs]
        s = qk_raw * (scale * jnp.exp(d_log - m_loc))
        w_inter = jnp.exp(inter_log - m_loc)
        num = _dot(s.astype(BF16), vh) + w_inter * inter_raw
        o_pre = _dot(h_ref[rs, :], win_ref[:, inner + hd * dh:inner + (hd + 1) * dh])
        n_vec = n_ref[hd:hd + 1, :]
        den = (jnp.sum(s, axis=-1, keepdims=True)
               + w_inter * jnp.sum(qh.astype(F32) * n_vec, axis=-1, keepdims=True))
        denom = jnp.maximum(jnp.abs(den), jnp.exp(-m_loc))
        hh = num * (1.0 / denom)
        ms = jnp.mean(hh * hh, axis=-1, keepdims=True)
        hn = hh * lax.rsqrt(ms + NORM_EPS) * mnorm_ref[:, hs] + skip_ref[:, hs] * xc_ref[rs, hs]
        gated = (jax.nn.sigmoid(o_pre) * hn).astype(BF16)
        part = _dot(gated, wout_ref[hs, :])
        mix = part if hd == 0 else mix + part
        f_last = f_col[cs - 1:cs, :]
        w_log = f_last - f_col + i_col
        m_new = jnp.maximum(f_last + m_prev, jnp.max(w_log, axis=0, keepdims=True))
        w_upd = jnp.exp(w_log - m_new)
        decay = jnp.exp(f_last + m_prev - m_new)
        kw = kh.astype(F32) * (scale * w_upd)
        pkw_ref[:, hs] = kw.astype(BF16)
        pv_ref[:, hs] = vh
        pdecay_ref[hd:hd + 1, :] = jnp.broadcast_to(decay, (1, pdecay_ref.shape[1]))
        n_ref[hd:hd + 1, :] = decay * n_vec + jnp.sum(kw, axis=0, keepdims=True)
        m_ref[hd:hd + 1, :] = jnp.broadcast_to(m_new, (1, m_ref.shape[1]))
        if hd == heads - 1:
            o_ref[rs, :] = x_ref[rs, :] + _rms(mix, lnpost_ref[...])


def _odd_layer(x, ln_pre, ln_post, w_in, conv_w, conv_b, w_q, w_k, w_v, w_gate, b_gate, m_norm, skip,
               w_out):
    bsz, seq, d = x.shape
    inner = w_out.shape[0]
    heads = b_gate.shape[0] // 2
    tile = min(ODD_TILE, seq)
    bw = MXU_DIM
    n_blocks = inner // bw
    per = bw // MLSTM_QKV_BLOCK
    on_diag = (jnp.arange(bw)[:, None] // MLSTM_QKV_BLOCK) == (jnp.arange(bw)[None, :] // MLSTM_QKV_BLOCK)

    def blockdiag(w):
        rows = w.astype(F32).reshape(n_blocks, bw, MLSTM_QKV_BLOCK)
        return jnp.where(on_diag, jnp.tile(rows, (1, 1, per)), 0.0)

    wqk = jnp.concatenate([blockdiag(w_q), blockdiag(w_k)], axis=2).astype(BF16)
    wv = blockdiag(w_v).astype(BF16)
    nb = inner // MLSTM_QKV_BLOCK
    wg_blocks = w_gate.astype(F32).reshape(3, nb, MLSTM_QKV_BLOCK, 2 * heads)

    def fold(w, g):
        return jnp.einsum('ncd,ndg->ncg', w.astype(F32), g,
                          precision=lax.Precision.HIGHEST).reshape(inner, 2 * heads)

    def pad_lanes(w):
        return jnp.zeros((inner, LANES), F32).at[:, :2 * heads].set(w).astype(BF16)

    wfqk = pad_lanes(fold(w_q, wg_blocks[0]) + fold(w_k, wg_blocks[1]))
    wfv = pad_lanes(fold(w_v, wg_blocks[2]))
    bg = jnp.zeros((1, LANES), F32).at[0, :2 * heads].set(b_gate)
    cw = jnp.zeros((SUBLANES, inner), F32).at[:conv_w.shape[0]].set(conv_w)
    args = (ln_pre.reshape(1, d), ln_post.reshape(1, d), w_in.astype(BF16), cw, conv_b.reshape(1, inner),
            wqk, wv, wfqk, wfv, bg, m_norm.reshape(1, inner), skip.reshape(1, inner), w_out.astype(BF16))
    dh = inner // heads
    scratch = [
        pltpu.VMEM((tile, d), BF16),
        pltpu.VMEM((tile, inner), F32),
        pltpu.VMEM((tile, inner), BF16),
        pltpu.VMEM((tile, inner), BF16),
        pltpu.VMEM((tile, inner), BF16),
        pltpu.VMEM((tile, inner), BF16),
        pltpu.VMEM((tile, inner), BF16),
        pltpu.VMEM((SUBLANES, inner), F32),
        pltpu.VMEM((heads, dh, dh), F32),
        pltpu.VMEM((SUBLANES, dh), F32),
        pltpu.VMEM((SUBLANES, LANES), F32),
        pltpu.VMEM((min(ODD_CHUNK, tile), inner), BF16),
        pltpu.VMEM((min(ODD_CHUNK, tile), inner), BF16),
        pltpu.VMEM((SUBLANES, LANES), F32),
    ]
    return pl.pallas_call(
        _odd_kernel,
        grid=(bsz, seq // tile),
        in_specs=[_tile_spec(tile, d)] + [_full_spec(a) for a in args],
        out_specs=_tile_spec(tile, d),
        out_shape=jax.ShapeDtypeStruct(x.shape, x.dtype),
        scratch_shapes=scratch,
        compiler_params=_compiler_params(),
        name="odd_mixer",
    )(x, *args)
```

```python
import functools

import jax
import jax.numpy as jnp
from jax import lax
from jax.experimental import pallas as pl
from jax.experimental.pallas import tpu as pltpu

F32 = jnp.float32
BF16 = jnp.bfloat16

NORM_EPS = 1e-6

S5_GROUP_SIZE = 16
GLA_HEADS = 4
GLA_CHUNK = 64
GLA_GATE_TEMP = 16.0
MLSTM_QKV_BLOCK = 4
MLSTM_CONV_TAPS = 4

LANES = 128
SUBLANES = 8
MXU_DIM = 256
VMEM_LIMIT_BYTES = 58 * 1024 * 1024

EVEN_TILE = 512
S5_STEPS = 64
ODD_TILE = 512
ODD_CHUNK = 256
FFN_TILE = 1024
FFN_COLS = 256
S5_LANE_GROUPS = LANES // S5_GROUP_SIZE
GLA_SUB = EVEN_TILE

def _dot(a, b):
    return jnp.dot(a, b, preferred_element_type=F32)


def _dot_nt(a, b):
    return lax.dot_general(a, b, (((1,), (1,)), ((), ())), preferred_element_type=F32)


def _dot_tn(a, b):
    return lax.dot_general(a, b, (((0,), (0,)), ((), ())), preferred_element_type=F32)


def _rms(x, g):
    ms = jnp.mean(x * x, axis=-1, keepdims=True)
    return x * lax.rsqrt(ms + NORM_EPS) * g


def _log_sigmoid(x):
    return jnp.minimum(x, 0.0) - jnp.log1p(jnp.exp(-jnp.abs(x)))


def _split_dot(mask_bf16, x):
    hi = x.astype(BF16)
    lo = (x - hi.astype(F32)).astype(BF16)
    return _dot(mask_bf16, hi) + _dot(mask_bf16, lo)


def _full_spec(arr):
    nd = arr.ndim
    return pl.BlockSpec(arr.shape, lambda b, l, _nd=nd: (0,) * _nd,
                        pipeline_mode=pl.Buffered(1))


def _tile_spec(tile, d):
    return pl.BlockSpec((None, tile, d), lambda b, l: (b, l, 0))


def _compiler_params():
    return pltpu.CompilerParams(dimension_semantics=("arbitrary", "arbitrary"),
                                vmem_limit_bytes=VMEM_LIMIT_BYTES)


def _ffn_kernel(x_ref, lnpre_ref, lnpost_ref, wup_ref, cw_ref, cb_ref, wd_ref,
                o_ref, h_ref, act_ref, tail_ref):
    tile = x_ref.shape[0]
    f = wd_ref.shape[0]
    n_chunks = tail_ref.shape[0]
    fc = tail_ref.shape[2]

    @pl.when(pl.program_id(1) == 0)
    def _():
        tail_ref[...] = jnp.zeros_like(tail_ref)

    h_ref[...] = _rms(x_ref[...], lnpre_ref[...]).astype(BF16)

    def conv3(g, w):
        return (w[0:1] * pltpu.roll(g, 2, axis=0) + w[1:2] * pltpu.roll(g, 1, axis=0)
                + w[2:3] * g)

    def up(c):
        h = h_ref[...]
        return (_dot(h, wup_ref[:, c * fc:(c + 1) * fc]),
                _dot(h, wup_ref[:, f + c * fc:f + (c + 1) * fc]))

    def gate(c, g, u):
        w = cw_ref[:, c * fc:(c + 1) * fc]
        b = cb_ref[:, c * fc:(c + 1) * fc]
        conv = conv3(g, w) + b
        head = jnp.concatenate([tail_ref[c], g[0:SUBLANES]], axis=0)
        conv_head = (conv3(head, w) + b)[SUBLANES:2 * SUBLANES]
        tail_ref[c] = g[tile - SUBLANES:tile]
        conv = jnp.concatenate([conv_head, conv[SUBLANES:]], axis=0)
        act_ref[:, c * fc:(c + 1) * fc] = (jax.nn.gelu(conv, approximate=True) * u).astype(BF16)

    gu = up(0)
    for c in range(n_chunks):
        gu_next = up(c + 1) if c + 1 < n_chunks else None
        gate(c, *gu)
        gu = gu_next
    o_ref[...] = x_ref[...] + _rms(_dot(act_ref[...], wd_ref[...]), lnpost_ref[...])


def _ffn_layer(x, ln_pre, ln_post, w_up, conv_w, conv_b, w_down):
    bsz, seq, d = x.shape
    f = w_down.shape[0]
    tile = min(FFN_TILE, seq)
    nc = f // FFN_COLS
    taps = conv_w.shape[0]
    cw = jnp.zeros((SUBLANES, f), F32).at[:taps].set(conv_w)
    args = (ln_pre.reshape(1, d), ln_post.reshape(1, d), w_up.astype(BF16), cw, conv_b.reshape(1, f),
            w_down.astype(BF16))
    return pl.pallas_call(
        _ffn_kernel,
        grid=(bsz, seq // tile),
        in_specs=[_tile_spec(tile, d)] + [_full_spec(a) for a in args],
        out_specs=_tile_spec(tile, d),
        out_shape=jax.ShapeDtypeStruct(x.shape, x.dtype),
        scratch_shapes=[pltpu.VMEM((tile, d), BF16), pltpu.VMEM((tile, f), BF16),
                        pltpu.VMEM((nc, SUBLANES, FFN_COLS), F32)],
        compiler_params=_compiler_params(),
        name="conv_ffn",
    )(x, *args)


def _even_kernel(x_ref, lnpre_ref, lnpost_ref, wmain_ref, wa_ref, wup_ref, balpha_ref, gnorm_ref,
                 lam_ref, lamc_ref, bmat_ref, cmat_ref, dskip_ref, wglu_ref, bglu_ref, wout_ref,
                 causalf_ref, causalb_ref, inchunk_ref, unperm_ref,
                 o_ref,
                 h_ref, unat_ref, uperm_ref, bu_ref, xs_ref, yperm_ref, e_ref, cin_ref, anat_ref,
                 s5carry_ref, glastate_ref):
    tile = x_ref.shape[0]
    n_blk = unat_ref.shape[0]
    sw = n_blk * LANES
    half = bu_ref.shape[2] // 2
    steps = S5_STEPS
    rows = tile // steps
    kw = wup_ref.shape[1]
    vw = gnorm_ref.shape[1]
    dk = kw // GLA_HEADS
    dv = vw // GLA_HEADS
    n_chunks = tile // GLA_CHUNK

    @pl.when(pl.program_id(1) == 0)
    def _():
        s5carry_ref[...] = jnp.zeros_like(s5carry_ref)
        glastate_ref[...] = jnp.zeros_like(glastate_ref)

    h_ref[...] = _rms(x_ref[...], lnpre_ref[...]).astype(BF16)
    h = h_ref[...]

    u = _dot(h, wmain_ref[:, 0:sw])
    for blk in range(n_blk):
        unat_ref[blk] = u[:, blk * LANES:(blk + 1) * LANES]
    for blk in range(n_blk):
        for s in range(steps):
            uperm_ref[s * rows:(s + 1) * rows, blk * LANES:(blk + 1) * LANES] = (
                unat_ref[blk, pl.ds(s, rows, stride=steps), :])

    def b_proj(blk):
        bu_ref[blk % 2] = _dot(uperm_ref[:, blk * LANES:(blk + 1) * LANES].astype(BF16),
                               bmat_ref[blk])

    def s5_block(blk):
        lanes = slice(blk * LANES, (blk + 1) * LANES)
        slot = blk % 2
        a_re = lam_ref[blk, 0:1, :]
        a_im = lam_ref[blk, 1:2, :]

        def advance(s, state, store):
            s_re, s_im = state
            b = bu_ref[slot, s * rows:(s + 1) * rows, :]
            n_re = a_re * s_re - a_im * s_im + b[:, :half]
            n_im = a_re * s_im + a_im * s_re + b[:, half:]
            if store:
                xs_ref[slot, s * rows:(s + 1) * rows, :] = jnp.concatenate([n_re, n_im], axis=1)
            return n_re, n_im

        state = (jnp.zeros((rows, half), F32), jnp.zeros((rows, half), F32))
        for s in range(steps):
            state = advance(s, state, False)
        e_ref[...] = jnp.concatenate(state, axis=1)
        ac_re = lamc_ref[blk, 0:1, :]
        ac_im = lamc_ref[blk, 1:2, :]
        c_re = s5carry_ref[blk, 0:1, :]
        c_im = s5carry_ref[blk, 1:2, :]
        for i in range(rows):
            cin_ref[i:i + 1, :] = jnp.concatenate([c_re, c_im], axis=1)
            e_row = e_ref[i:i + 1, :]
            c_re, c_im = (ac_re * c_re - ac_im * c_im + e_row[:, :half],
                          ac_re * c_im + ac_im * c_re + e_row[:, half:])
        s5carry_ref[blk, 0:1, :] = c_re
        s5carry_ref[blk, 1:2, :] = c_im
        cin = cin_ref[...]
        state = (cin[:, :half], cin[:, half:])
        for s in range(steps):
            state = advance(s, state, True)
        yperm_ref[:, lanes] = _dot(xs_ref[slot].astype(BF16), cmat_ref[blk])

    for blk in range(min(2, n_blk)):
        b_proj(blk)

    o0 = sw
    r0 = o0 + 2 * kw + vw
    q = _dot(h, wmain_ref[:, o0:o0 + kw]) * (dk ** -0.5)
    k = _dot(h, wmain_ref[:, o0 + kw:o0 + 2 * kw])
    v = _dot(h, wmain_ref[:, o0 + 2 * kw:o0 + 2 * kw + vw]).astype(BF16)
    a_lr = _dot(h, wa_ref[...])
    z = _dot(a_lr.astype(BF16), wup_ref[...]) + balpha_ref[...]
    log_a = _log_sigmoid(z) * (1.0 / GLA_GATE_TEMP)

    sub = causalf_ref.shape[0]
    subs = [slice(i * sub, (i + 1) * sub) for i in range(tile // sub)]
    g = jnp.concatenate([_split_dot(causalb_ref[...], log_a[rs]) for rs in subs], axis=0)
    g_last = jnp.concatenate([_split_dot(inchunk_ref[...], log_a[rs]) for rs in subs], axis=0)
    q_dec = (q * jnp.exp(g)).astype(BF16)
    k_dec = (k * jnp.exp(-g)).astype(BF16)
    k_upd = (k * jnp.exp(g_last - g)).astype(BF16)
    decay = jnp.exp(g_last)
    keep = causalf_ref[...] > 0.5
    gate = jax.nn.silu(_dot(h, wmain_ref[:, r0:r0 + vw]))

    def gla_head(hd):
        ks = slice(hd * dk, (hd + 1) * dk)
        vs = slice(hd * dv, (hd + 1) * dv)
        qh, kh, kuh, vh = q_dec[:, ks], k_dec[:, ks], k_upd[:, ks], v[:, vs]
        o_intra = jnp.concatenate(
            [_dot(jnp.where(keep, _dot_nt(qh[rs], kh[rs]), 0.0).astype(BF16), vh[rs]) for rs in subs],
            axis=0)
        chunks = [slice(c * GLA_CHUNK, (c + 1) * GLA_CHUNK) for c in range(n_chunks)]
        d_st = [_dot_tn(vh[rs], kuh[rs]) for rs in chunks]
        st = glastate_ref[hd]
        states = []
        for c in range(n_chunks):
            states.append(st.astype(BF16))
            st = st * decay[c * GLA_CHUNK:c * GLA_CHUNK + 1, ks] + d_st[c]
        glastate_ref[hd] = st
        oh = jnp.concatenate([o_intra[rs] + _dot_nt(qh[rs], states[c]) for c, rs in enumerate(chunks)],
                             axis=0)
        ms = jnp.mean(oh * oh, axis=-1, keepdims=True)
        anat_ref[:, sw + hd * dv:sw + (hd + 1) * dv] = (
            oh * lax.rsqrt(ms + NORM_EPS) * gnorm_ref[:, vs] * gate[:, vs]).astype(BF16)

    s5_block(0)
    for i in range(max(n_blk - 1, GLA_HEADS)):
        if i + 2 < n_blk:
            b_proj(i + 2)
        if i < GLA_HEADS:
            gla_head(i)
        if i + 1 < n_blk:
            s5_block(i + 1)

    y = yperm_ref[...] + dskip_ref[...] * uperm_ref[...]
    y = jax.nn.gelu(y, approximate=True)
    a_perm = y * jax.nn.sigmoid(_dot(y.astype(BF16), wglu_ref[...]) + bglu_ref[...])
    anat_ref[:, 0:sw] = _dot(unperm_ref[...], a_perm.astype(BF16)).astype(BF16)

    o_ref[...] = x_ref[...] + _rms(_dot(anat_ref[...], wout_ref[...]), lnpost_ref[...])


def _s5_discretize(lam_re, lam_im, b_re, b_im, log_dt, power):
    dt = jnp.exp(log_dt)[:, None]
    mag = jnp.exp(lam_re * dt)
    lb_re = mag * jnp.cos(lam_im * dt)
    lb_im = mag * jnp.sin(lam_im * dt)
    inv = 1.0 / (lam_re * lam_re + lam_im * lam_im)
    zr = ((lb_re - 1.0) * lam_re + lb_im * lam_im) * inv
    zi = (lb_im * lam_re - (lb_re - 1.0) * lam_im) * inv
    bb_re = zr[..., None] * b_re - zi[..., None] * b_im
    bb_im = zr[..., None] * b_im + zi[..., None] * b_re
    magp = jnp.exp(lam_re * dt * power)
    lp_re = magp * jnp.cos(lam_im * dt * power)
    lp_im = magp * jnp.sin(lam_im * dt * power)
    return lb_re, lb_im, lp_re, lp_im, bb_re, bb_im


def _even_layer(x, ln_pre, ln_post, w_in, lam_re, lam_im, b_re, b_im, c_re, c_im, d_skip, log_dt,
                w_glu, b_glu, w_alpha_up, b_alpha, gla_norm, w_out):
    bsz, seq, d = x.shape
    groups, state = lam_re.shape
    gs = b_re.shape[-1]
    sw = groups * gs
    rank, kw = w_alpha_up.shape
    vw = gla_norm.shape[0]
    tile = min(EVEN_TILE, seq)
    lg = S5_LANE_GROUPS
    n_blk = groups // lg
    half = lg * state

    lb_re, lb_im, lp_re, lp_im, bb_re, bb_im = _s5_discretize(
        lam_re.astype(F32), lam_im.astype(F32), b_re.astype(F32), b_im.astype(F32),
        log_dt.astype(F32), float(S5_STEPS))

    def lam_pack(re, im):
        out = jnp.zeros((n_blk, SUBLANES, half), F32)
        return out.at[:, 0].set(re.reshape(n_blk, half)).at[:, 1].set(im.reshape(n_blk, half))

    same_group = (jnp.arange(lg * gs)[:, None] // gs) == (jnp.arange(half)[None, :] // state)

    def b_pack(bb):
        rows = bb.transpose(0, 2, 1).reshape(n_blk, lg * gs, state)
        return jnp.where(same_group, jnp.tile(rows, (1, 1, lg)), 0.0)

    def c_pack(cc):
        rows = cc.transpose(0, 2, 1).reshape(n_blk, half, gs)
        return jnp.where(same_group.T, jnp.tile(rows, (1, 1, lg)), 0.0)

    bmat = jnp.concatenate([b_pack(bb_re), b_pack(bb_im)], axis=2).astype(BF16)
    cmat = jnp.concatenate([c_pack(c_re.astype(F32)), -c_pack(c_im.astype(F32))], axis=1).astype(BF16)
    main = sw + 2 * kw + 2 * vw
    wa = jnp.zeros((d, LANES), F32).at[:, :rank].set(w_in[:, main:main + rank]).astype(BF16)
    wup = jnp.zeros((LANES, kw), F32).at[:rank].set(w_alpha_up).astype(BF16)
    args = (ln_pre.reshape(1, d), ln_post.reshape(1, d), w_in[:, :main].astype(BF16), wa, wup,
            b_alpha.reshape(1, kw), gla_norm.reshape(1, vw),
            lam_pack(lb_re, lb_im), lam_pack(lp_re, lp_im), bmat, cmat,
            d_skip.reshape(1, sw), w_glu.astype(BF16), b_glu.reshape(1, sw), w_out.astype(BF16))
    rows = tile // S5_STEPS
    r_idx = jnp.arange(tile, dtype=jnp.int32)[:, None]
    c_idx = jnp.arange(tile, dtype=jnp.int32)[None, :]
    unperm = (c_idx == (r_idx % S5_STEPS) * rows + r_idx // S5_STEPS)
    sub = min(GLA_SUB, tile)
    r_idx, c_idx = r_idx[:sub], c_idx[:, :sub]
    in_chunk = (r_idx // GLA_CHUNK == c_idx // GLA_CHUNK)
    causal = (in_chunk & (c_idx <= r_idx)).astype(F32)
    args = args + (causal, causal.astype(BF16), in_chunk.astype(BF16), unperm.astype(BF16))
    scratch = [
        pltpu.VMEM((tile, d), BF16),
        pltpu.VMEM((n_blk, tile, LANES), F32),
        pltpu.VMEM((tile, sw), F32),
        pltpu.VMEM((2, tile, 2 * half), F32),
        pltpu.VMEM((2, tile, 2 * half), F32),
        pltpu.VMEM((tile, sw), F32),
        pltpu.VMEM((rows, 2 * half), F32),
        pltpu.VMEM((rows, 2 * half), F32),
        pltpu.VMEM((tile, sw + vw), BF16),
        pltpu.VMEM((n_blk, SUBLANES, half), F32),
        pltpu.VMEM((GLA_HEADS, vw // GLA_HEADS, kw // GLA_HEADS), F32),
    ]
    return pl.pallas_call(
        _even_kernel,
        grid=(bsz, seq // tile),
        in_specs=[_tile_spec(tile, d)] + [_full_spec(a) for a in args],
        out_specs=_tile_spec(tile, d),
        out_shape=jax.ShapeDtypeStruct(x.shape, x.dtype),
        scratch_shapes=scratch,
        compiler_params=_compiler_params(),
        name="even_mixer",
    )(x, *args)


def kernel(x, ln_mix_pre, ln_mix_post, ln_ffn_pre, ln_ffn_post, ev_w_in, s5_lambda_re, s5_lambda_im, s5_b_re, s5_b_im, s5_c_re, s5_c_im, s5_d, s5_log_dt, s5_w_glu, s5_b_glu, gla_w_alpha_up, gla_b_alpha, gla_norm, ev_w_out, od_w_in, ml_conv_w, ml_conv_b, ml_w_q, ml_w_k, ml_w_v, ml_w_gate, ml_b_gate, ml_norm, ml_skip, od_w_out, ffn_w_up, ffn_conv_w, ffn_conv_b, ffn_w_down):
    depth = ln_mix_pre.shape[0]
    for layer in range(depth):
        if layer % 2 == 0:
            e = layer // 2
            x = _even_layer(x, ln_mix_pre[layer], ln_mix_post[layer], ev_w_in[e], s5_lambda_re[e],
                            s5_lambda_im[e], s5_b_re[e], s5_b_im[e], s5_c_re[e], s5_c_im[e], s5_d[e],
                            s5_log_dt[e], s5_w_glu[e], s5_b_glu[e], gla_w_alpha_up[e], gla_b_alpha[e],
                            gla_norm[e], ev_w_out[e])
        else:
            o = layer // 2
            x = _odd_layer(x, ln_mix_pre[layer], ln_mix_post[layer], od_w_in[o], ml_conv_w[o],
                           ml_conv_b[o], ml_w_q[o], ml_w_k[o], ml_w_v[o], ml_w_gate[o], ml_b_gate[o],
                           ml_norm[o], ml_skip[o], od_w_out[o])
        x = _ffn_layer(x, ln_ffn_pre[layer], ln_ffn_post[layer], ffn_w_up[layer], ffn_conv_w[layer],
                       ffn_conv_b[layer], ffn_w_down[layer])
    return x


def _odd_kernel(x_ref, lnpre_ref, lnpost_ref, win_ref, cw_ref, cb_ref, wqk_ref, wv_ref, wfqk_ref, wfv_ref,
                bg_ref,
                mnorm_ref, skip_ref, wout_ref,
                o_ref,
                h_ref, xc_ref, xcb_ref, xmb_ref, q_ref, k_ref, v_ref, tail_ref, c_ref, n_ref,
                m_ref, pkw_ref, pv_ref, pdecay_ref):
    tile = x_ref.shape[0]
    inner = xc_ref.shape[1]
    heads = c_ref.shape[0]
    dh = inner // heads
    bw = wqk_ref.shape[1]
    taps = MLSTM_CONV_TAPS

    @pl.when(pl.program_id(1) == 0)
    def _():
        tail_ref[...] = jnp.zeros_like(tail_ref)
        c_ref[...] = jnp.zeros_like(c_ref)
        n_ref[...] = jnp.zeros_like(n_ref)
        m_ref[...] = jnp.zeros_like(m_ref)
        pkw_ref[...] = jnp.zeros_like(pkw_ref)
        pv_ref[...] = jnp.zeros_like(pv_ref)
        pdecay_ref[...] = jnp.zeros_like(pdecay_ref)

    def apply_pending(hd):
        hs = slice(hd * dh, (hd + 1) * dh)
        c_ref[hd] = (pdecay_ref[hd:hd + 1, 0:1] * c_ref[hd]
                     + _dot_tn(pkw_ref[:, hs], pv_ref[:, hs]))

    for hd in range(heads):
        apply_pending(hd)

    h_ref[...] = _rms(x_ref[...], lnpre_ref[...]).astype(BF16)

    def conv(xm, w):
        out = w[taps - 1:taps] * xm
        for kk in range(taps - 1):
            out = out + w[kk:kk + 1] * pltpu.roll(xm, taps - 1 - kk, axis=0)
        return out

    gw = 2 * bw
    n_groups = inner // gw

    def in_proj(grp):
        return _dot(h_ref[...], win_ref[:, grp * gw:(grp + 1) * gw])

    def conv_silu(grp, xm):
        ls = slice(grp * gw, (grp + 1) * gw)
        w = cw_ref[:, ls]
        b = cb_ref[:, ls]
        pre = conv(xm, w) + b
        head = jnp.concatenate([tail_ref[:, ls], xm[0:SUBLANES]], axis=0)
        pre_head = (conv(head, w) + b)[SUBLANES:2 * SUBLANES]
        tail_ref[:, ls] = xm[tile - SUBLANES:tile]
        xc = jax.nn.silu(jnp.concatenate([pre_head, pre[SUBLANES:]], axis=0))
        xc_ref[:, ls] = xc
        xcb_ref[:, ls] = xc.astype(BF16)
        xmb_ref[:, ls] = xm.astype(BF16)

    def headwise(grp):
        for blk in range(grp * (gw // bw), (grp + 1) * (gw // bw)):
            ls = slice(blk * bw, (blk + 1) * bw)
            qk = _dot(xcb_ref[:, ls], wqk_ref[blk])
            q_ref[:, ls] = qk[:, :bw].astype(BF16)
            k_ref[:, ls] = qk[:, bw:].astype(BF16)
            v_ref[:, ls] = _dot(xmb_ref[:, ls], wv_ref[blk]).astype(BF16)
        gs = slice(grp * gw, (grp + 1) * gw)
        gate_parts.append(_dot(xcb_ref[:, gs], wfqk_ref[gs, :]) + _dot(xmb_ref[:, gs], wfv_ref[gs, :]))

    gate_parts = []

    xm_next = in_proj(0)
    for grp in range(n_groups):
        xm = xm_next
        if grp + 1 < n_groups:
            xm_next = in_proj(grp + 1)
        if grp > 0:
            headwise(grp - 1)
        conv_silu(grp, xm)
    headwise(n_groups - 1)

    cs = min(ODD_CHUNK, tile)
    items = [(ck, hd) for ck in range(tile // cs) for hd in range(heads)]

    def state_matmuls(ck, hd):
        rs = slice(ck * cs, (ck + 1) * cs)
        hs = slice(hd * dh, (hd + 1) * dh)
        qh = q_ref[rs, hs]
        if ck > 0:
            apply_pending(hd)
        return _dot_nt(qh, k_ref[rs, hs]), _dot(qh, c_ref[hd].astype(BF16))

    ahead = state_matmuls(*items[0])

    gates_all = bg_ref[...] + functools.reduce(jnp.add, gate_parts)
    log_f = _log_sigmoid(gates_all)
    row = lax.broadcasted_iota(jnp.int32, (cs, cs), 0)
    col = lax.broadcasted_iota(jnp.int32, (cs, cs), 1)
    causal = col <= row
    tri = jnp.where(causal, 1.0, 0.0).astype(BF16)
    chunk_gates = []
    for ck in range(tile // cs):
        rs = slice(ck * cs, (ck + 1) * cs)
        g_ck = gates_all[rs]
        f_ck = _split_dot(tri, log_f[rs])
        chunk_gates.append((g_ck, f_ck, jnp.transpose(g_ck), jnp.transpose(f_ck)))
    scale = dh ** -0.5

    for idx, (ck, hd) in enumerate(items):
        qk_raw, inter_raw = ahead
        if idx + 1 < len(items):
            ahead = state_matmuls(*items[idx + 1])
        rs = slice(ck * cs, (ck + 1) * cs)
        hs = slice(hd * dh, (hd + 1) * dh)
        gates, f_cum, gates_t, f_cum_t = chunk_gates[ck]
        f_col = f_cum[:, heads + hd:heads + hd + 1]
        f_row = f_cum_t[heads + hd:heads + hd + 1, :]
        i_col = gates[:, hd:hd + 1]
        i_row = gates_t[hd:hd + 1, :]
        m_prev = m_ref[hd:hd + 1, 0:1]
        d_log = jnp.where(causal, f_col - f_row + i_row, -jnp.inf)
        inter_log = f_col + m_prev
        m_loc = jnp.maximum(inter_log, jnp.max(d_log, axis=-1, keepdims=True))
        qh = q_ref[rs, hs]
        kh = k_ref[rs, hs]
        vh = v_ref[rs, hs]
        s = qk_raw * (scale * jnp.exp(d_log - m_loc))
        w_inter = jnp.exp(inter_log - m_loc)
        num = _dot(s.astype(BF16), vh) + w_inter * inter_raw
        o_pre = _dot(h_ref[rs, :], win_ref[:, inner + hd * dh:inner + (hd + 1) * dh])
        n_vec = n_ref[hd:hd + 1, :]
        den = (jnp.sum(s, axis=-1, keepdims=True)
               + w_inter * jnp.sum(qh.astype(F32) * n_vec, axis=-1, keepdims=True))
        denom = jnp.maximum(jnp.abs(den), jnp.exp(-m_loc))
        hh = num * (1.0 / denom)
        ms = jnp.mean(hh * hh, axis=-1, keepdims=True)
        hn = hh * lax.rsqrt(ms + NORM_EPS) * mnorm_ref[:, hs] + skip_ref[:, hs] * xc_ref[rs, hs]
        gated = (jax.nn.sigmoid(o_pre) * hn).astype(BF16)
        part = _dot(gated, wout_ref[hs, :])
        mix = part if hd == 0 else mix + part
        f_last = f_col[cs - 1:cs, :]
        w_log = f_last - f_col + i_col
        m_new = jnp.maximum(f_last + m_prev, jnp.max(w_log, axis=0, keepdims=True))
        w_upd = jnp.exp(w_log - m_new)
        decay = jnp.exp(f_last + m_prev - m_new)
        kw = kh.astype(F32) * (scale * w_upd)
        pkw_ref[:, hs] = kw.astype(BF16)
        pv_ref[:, hs] = vh
        pdecay_ref[hd:hd + 1, :] = jnp.broadcast_to(decay, (1, pdecay_ref.shape[1]))
        n_ref[hd:hd + 1, :] = decay * n_vec + jnp.sum(kw, axis=0, keepdims=True)
        m_ref[hd:hd + 1, :] = jnp.broadcast_to(m_new, (1, m_ref.shape[1]))
        if hd == heads - 1:
            o_ref[rs, :] = x_ref[rs, :] + _rms(mix, lnpost_ref[...])


def _odd_layer(x, ln_pre, ln_post, w_in, conv_w, conv_b, w_q, w_k, w_v, w_gate, b_gate, m_norm, skip,
               w_out):
    bsz, seq, d = x.shape
    inner = w_out.shape[0]
    heads = b_gate.shape[0] // 2
    tile = min(ODD_TILE, seq)
    bw = MXU_DIM
    n_blocks = inner // bw
    per = bw // MLSTM_QKV_BLOCK
    on_diag = (jnp.arange(bw)[:, None] // MLSTM_QKV_BLOCK) == (jnp.arange(bw)[None, :] // MLSTM_QKV_BLOCK)

    def blockdiag(w):
        rows = w.astype(F32).reshape(n_blocks, bw, MLSTM_QKV_BLOCK)
        return jnp.where(on_diag, jnp.tile(rows, (1, 1, per)), 0.0)

    wqk = jnp.concatenate([blockdiag(w_q), blockdiag(w_k)], axis=2).astype(BF16)
    wv = blockdiag(w_v).astype(BF16)
    nb = inner // MLSTM_QKV_BLOCK
    wg_blocks = w_gate.astype(F32).reshape(3, nb, MLSTM_QKV_BLOCK, 2 * heads)

    def fold(w, g):
        return jnp.einsum('ncd,ndg->ncg', w.astype(F32), g,
                          precision=lax.Precision.HIGHEST).reshape(inner, 2 * heads)

    def pad_lanes(w):
        return jnp.zeros((inner, LANES), F32).at[:, :2 * heads].set(w).astype(BF16)

    wfqk = pad_lanes(fold(w_q, wg_blocks[0]) + fold(w_k, wg_blocks[1]))
    wfv = pad_lanes(fold(w_v, wg_blocks[2]))
    bg = jnp.zeros((1, LANES), F32).at[0, :2 * heads].set(b_gate)
    cw = jnp.zeros((SUBLANES, inner), F32).at[:conv_w.shape[0]].set(conv_w)
    args = (ln_pre.reshape(1, d), ln_post.reshape(1, d), w_in.astype(BF16), cw, conv_b.reshape(1, inner),
            wqk, wv, wfqk, wfv, bg, m_norm.reshape(1, inner), skip.reshape(1, inner), w_out.astype(BF16))
    dh = inner // heads
    scratch = [
        pltpu.VMEM((tile, d), BF16),
        pltpu.VMEM((tile, inner), F32),
        pltpu.VMEM((tile, inner), BF16),
        pltpu.VMEM((tile, inner), BF16),
        pltpu.VMEM((tile, inner), BF16),
        pltpu.VMEM((tile, inner), BF16),
        pltpu.VMEM((tile, inner), BF16),
        pltpu.VMEM((SUBLANES, inner), F32),
        pltpu.VMEM((heads, dh, dh), F32),
        pltpu.VMEM((SUBLANES, dh), F32),
        pltpu.VMEM((SUBLANES, LANES), F32),
        pltpu.VMEM((min(ODD_CHUNK, tile), inner), BF16),
        pltpu.VMEM((min(ODD_CHUNK, tile), inner), BF16),
        pltpu.VMEM((SUBLANES, LANES), F32),
    ]
    return pl.pallas_call(
        _odd_kernel,
        grid=(bsz, seq // tile),
        in_specs=[_tile_spec(tile, d)] + [_full_spec(a) for a in args],
        out_specs=_tile_spec(tile, d),
        out_shape=jax.ShapeDtypeStruct(x.shape, x.dtype),
        scratch_shapes=scratch,
        compiler_params=_compiler_params(),
        name="odd_mixer",
    )(x, *args)
```

```python
import functools

import jax
import jax.numpy as jnp
from jax import lax
from jax.experimental import pallas as pl
from jax.experimental.pallas import tpu as pltpu

F32 = jnp.float32
BF16 = jnp.bfloat16

NORM_EPS = 1e-6

S5_GROUP_SIZE = 16
GLA_HEADS = 4
GLA_CHUNK = 64
GLA_GATE_TEMP = 16.0
MLSTM_QKV_BLOCK = 4
MLSTM_CONV_TAPS = 4

LANES = 128
SUBLANES = 8
MXU_DIM = 256
VMEM_LIMIT_BYTES = 58 * 1024 * 1024

EVEN_TILE = 512
S5_STEPS = 64
ODD_TILE = 512
ODD_CHUNK = 256
FFN_TILE = 1024
FFN_COLS = 256
S5_LANE_GROUPS = LANES // S5_GROUP_SIZE
GLA_SUB = EVEN_TILE

def _dot(a, b):
    return jnp.dot(a, b, preferred_element_type=F32)


def _dot_nt(a, b):
    return lax.dot_general(a, b, (((1,), (1,)), ((), ())), preferred_element_type=F32)


def _dot_tn(a, b):
    return lax.dot_general(a, b, (((0,), (0,)), ((), ())), preferred_element_type=F32)


def _rms(x, g):
    ms = jnp.mean(x * x, axis=-1, keepdims=True)
    return x * lax.rsqrt(ms + NORM_EPS) * g


def _log_sigmoid(x):
    return jnp.minimum(x, 0.0) - jnp.log1p(jnp.exp(-jnp.abs(x)))


def _split_dot(mask_bf16, x):
    hi = x.astype(BF16)
    lo = (x - hi.astype(F32)).astype(BF16)
    return _dot(mask_bf16, hi) + _dot(mask_bf16, lo)


def _full_spec(arr):
    nd = arr.ndim
    return pl.BlockSpec(arr.shape, lambda b, l, _nd=nd: (0,) * _nd,
                        pipeline_mode=pl.Buffered(1))


def _tile_spec(tile, d):
    return pl.BlockSpec((None, tile, d), lambda b, l: (b, l, 0))


def _compiler_params():
    return pltpu.CompilerParams(dimension_semantics=("arbitrary", "arbitrary"),
                                vmem_limit_bytes=VMEM_LIMIT_BYTES)


def _ffn_kernel(x_ref, lnpre_ref, lnpost_ref, wup_ref, cw_ref, cb_ref, wd_ref,
                o_ref, h_ref, act_ref, tail_ref):
    tile = x_ref.shape[0]
    f = wd_ref.shape[0]
    n_chunks = tail_ref.shape[0]
    fc = tail_ref.shape[2]

    @pl.when(pl.program_id(1) == 0)
    def _():
        tail_ref[...] = jnp.zeros_like(tail_ref)

    h_ref[...] = _rms(x_ref[...], lnpre_ref[...]).astype(BF16)

    def conv3(g, w):
        return (w[0:1] * pltpu.roll(g, 2, axis=0) + w[1:2] * pltpu.roll(g, 1, axis=0)
                + w[2:3] * g)

    def up(c):
        h = h_ref[...]
        return (_dot(h, wup_ref[:, c * fc:(c + 1) * fc]),
                _dot(h, wup_ref[:, f + c * fc:f + (c + 1) * fc]))

    def gate(c, g, u):
        w = cw_ref[:, c * fc:(c + 1) * fc]
        b = cb_ref[:, c * fc:(c + 1) * fc]
        conv = conv3(g, w) + b
        head = jnp.concatenate([tail_ref[c], g[0:SUBLANES]], axis=0)
        conv_head = (conv3(head, w) + b)[SUBLANES:2 * SUBLANES]
        tail_ref[c] = g[tile - SUBLANES:tile]
        conv = jnp.concatenate([conv_head, conv[SUBLANES:]], axis=0)
        act_ref[:, c * fc:(c + 1) * fc] = (jax.nn.gelu(conv, approximate=True) * u).astype(BF16)

    gu = up(0)
    for c in range(n_chunks):
        gu_next = up(c + 1) if c + 1 < n_chunks else None
        gate(c, *gu)
        gu = gu_next
    o_ref[...] = x_ref[...] + _rms(_dot(act_ref[...], wd_ref[...]), lnpost_ref[...])


def _ffn_layer(x, ln_pre, ln_post, w_up, conv_w, conv_b, w_down):
    bsz, seq, d = x.shape
    f = w_down.shape[0]
    tile = min(FFN_TILE, seq)
    nc = f // FFN_COLS
    taps = conv_w.shape[0]
    cw = jnp.zeros((SUBLANES, f), F32).at[:taps].set(conv_w)
    args = (ln_pre.reshape(1, d), ln_post.reshape(1, d), w_up.astype(BF16), cw, conv_b.reshape(1, f),
            w_down.astype(BF16))
    return pl.pallas_call(
        _ffn_kernel,
        grid=(bsz, seq // tile),
        in_specs=[_tile_spec(tile, d)] + [_full_spec(a) for a in args],
        out_specs=_tile_spec(tile, d),
        out_shape=jax.ShapeDtypeStruct(x.shape, x.dtype),
        scratch_shapes=[pltpu.VMEM((tile, d), BF16), pltpu.VMEM((tile, f), BF16),
                        pltpu.VMEM((nc, SUBLANES, FFN_COLS), F32)],
        compiler_params=_compiler_params(),
        name="conv_ffn",
    )(x, *args)


def _even_kernel(x_ref, lnpre_ref, lnpost_ref, wmain_ref, wa_ref, wup_ref, balpha_ref, gnorm_ref,
                 lam_ref, lamc_ref, bmat_ref, cmat_ref, dskip_ref, wglu_ref, bglu_ref, wout_ref,
                 causalf_ref, causalb_ref, inchunk_ref, unperm_ref,
                 o_ref,
                 h_ref, unat_ref, uperm_ref, bu_ref, xs_ref, yperm_ref, e_ref, cin_ref, anat_ref,
                 s5carry_ref, glastate_ref):
    tile = x_ref.shape[0]
    n_blk = unat_ref.shape[0]
    sw = n_blk * LANES
    half = bu_ref.shape[2] // 2
    steps = S5_STEPS
    rows = tile // steps
    kw = wup_ref.shape[1]
    vw = gnorm_ref.shape[1]
    dk = kw // GLA_HEADS
    dv = vw // GLA_HEADS
    n_chunks = tile // GLA_CHUNK

    @pl.when(pl.program_id(1) == 0)
    def _():
        s5carry_ref[...] = jnp.zeros_like(s5carry_ref)
        glastate_ref[...] = jnp.zeros_like(glastate_ref)

    h_ref[...] = _rms(x_ref[...], lnpre_ref[...]).astype(BF16)
    h = h_ref[...]

    u = _dot(h, wmain_ref[:, 0:sw])
    for blk in range(n_blk):
        unat_ref[blk] = u[:, blk * LANES:(blk + 1) * LANES]
    for blk in range(n_blk):
        for s in range(steps):
            uperm_ref[s * rows:(s + 1) * rows, blk * LANES:(blk + 1) * LANES] = (
                unat_ref[blk, pl.ds(s, rows, stride=steps), :])

    def b_proj(blk):
        bu_ref[blk % 2] = _dot(uperm_ref[:, blk * LANES:(blk + 1) * LANES].astype(BF16),
                               bmat_ref[blk])

    def s5_block(blk):
        lanes = slice(blk * LANES, (blk + 1) * LANES)
        slot = blk % 2
        a_re = lam_ref[blk, 0:1, :]
        a_im = lam_ref[blk, 1:2, :]

        def advance(s, state, store):
            s_re, s_im = state
            b = bu_ref[slot, s * rows:(s + 1) * rows, :]
            n_re = a_re * s_re - a_im * s_im + b[:, :half]
            n_im = a_re * s_im + a_im * s_re + b[:, half:]
            if store:
                xs_ref[slot, s * rows:(s + 1) * rows, :] = jnp.concatenate([n_re, n_im], axis=1)
            return n_re, n_im

        state = (jnp.zeros((rows, half), F32), jnp.zeros((rows, half), F32))
        for s in range(steps):
            state = advance(s, state, False)
        e_ref[...] = jnp.concatenate(state, axis=1)
        ac_re = lamc_ref[blk, 0:1, :]
        ac_im = lamc_ref[blk, 1:2, :]
        c_re = s5carry_ref[blk, 0:1, :]
        c_im = s5carry_ref[blk, 1:2, :]
        for i in range(rows):
            cin_ref[i:i + 1, :] = jnp.concatenate([c_re, c_im], axis=1)
            e_row = e_ref[i:i + 1, :]
            c_re, c_im = (ac_re * c_re - ac_im * c_im + e_row[:, :half],
                          ac_re * c_im + ac_im * c_re + e_row[:, half:])
        s5carry_ref[blk, 0:1, :] = c_re
        s5carry_ref[blk, 1:2, :] = c_im
        cin = cin_ref[...]
        state = (cin[:, :half], cin[:, half:])
        for s in range(steps):
            state = advance(s, state, True)
        yperm_ref[:, lanes] = _dot(xs_ref[slot].astype(BF16), cmat_ref[blk])

    for blk in range(min(2, n_blk)):
        b_proj(blk)

    o0 = sw
    r0 = o0 + 2 * kw + vw
    q = _dot(h, wmain_ref[:, o0:o0 + kw]) * (dk ** -0.5)
    k = _dot(h, wmain_ref[:, o0 + kw:o0 + 2 * kw])
    v = _dot(h, wmain_ref[:, o0 + 2 * kw:o0 + 2 * kw + vw]).astype(BF16)
    a_lr = _dot(h, wa_ref[...])
    z = _dot(a_lr.astype(BF16), wup_ref[...]) + balpha_ref[...]
    log_a = _log_sigmoid(z) * (1.0 / GLA_GATE_TEMP)

    sub = causalf_ref.shape[0]
    subs = [slice(i * sub, (i + 1) * sub) for i in range(tile // sub)]
    g = jnp.concatenate([_split_dot(causalb_ref[...], log_a[rs]) for rs in subs], axis=0)
    g_last = jnp.concatenate([_split_dot(inchunk_ref[...], log_a[rs]) for rs in subs], axis=0)
    q_dec = (q * jnp.exp(g)).astype(BF16)
    k_dec = (k * jnp.exp(-g)).astype(BF16)
    k_upd = (k * jnp.exp(g_last - g)).astype(BF16)
    decay = jnp.exp(g_last)
    keep = causalf_ref[...] > 0.5
    gate = jax.nn.silu(_dot(h, wmain_ref[:, r0:r0 + vw]))

    def gla_head(hd):
        ks = slice(hd * dk, (hd + 1) * dk)
        vs = slice(hd * dv, (hd + 1) * dv)
        qh, kh, kuh, vh = q_dec[:, ks], k_dec[:, ks], k_upd[:, ks], v[:, vs]
        o_intra = jnp.concatenate(
            [_dot(jnp.where(keep, _dot_nt(qh[rs], kh[rs]), 0.0).astype(BF16), vh[rs]) for rs in subs],
            axis=0)
        chunks = [slice(c * GLA_CHUNK, (c + 1) * GLA_CHUNK) for c in range(n_chunks)]
        d_st = [_dot_tn(vh[rs], kuh[rs]) for rs in chunks]
        st = glastate_ref[hd]
        states = []
        for c in range(n_chunks):
            states.append(st.astype(BF16))
            st = st * decay[c * GLA_CHUNK:c * GLA_CHUNK + 1, ks] + d_st[c]
        glastate_ref[hd] = st
        oh = jnp.concatenate([o_intra[rs] + _dot_nt(qh[rs], states[c]) for c, rs in enumerate(chunks)],
                             axis=0)
        ms = jnp.mean(oh * oh, axis=-1, keepdims=True)
        anat_ref[:, sw + hd * dv:sw + (hd + 1) * dv] = (
            oh * lax.rsqrt(ms + NORM_EPS) * gnorm_ref[:, vs] * gate[:, vs]).astype(BF16)

    s5_block(0)
    for i in range(max(n_blk - 1, GLA_HEADS)):
        if i + 2 < n_blk:
            b_proj(i + 2)
        if i < GLA_HEADS:
            gla_head(i)
        if i + 1 < n_blk:
            s5_block(i + 1)

    y = yperm_ref[...] + dskip_ref[...] * uperm_ref[...]
    y = jax.nn.gelu(y, approximate=True)
    a_perm = y * jax.nn.sigmoid(_dot(y.astype(BF16), wglu_ref[...]) + bglu_ref[...])
    anat_ref[:, 0:sw] = _dot(unperm_ref[...], a_perm.astype(BF16)).astype(BF16)

    o_ref[...] = x_ref[...] + _rms(_dot(anat_ref[...], wout_ref[...]), lnpost_ref[...])


def _s5_discretize(lam_re, lam_im, b_re, b_im, log_dt, power):
    dt = jnp.exp(log_dt)[:, None]
    mag = jnp.exp(lam_re * dt)
    lb_re = mag * jnp.cos(lam_im * dt)
    lb_im = mag * jnp.sin(lam_im * dt)
    inv = 1.0 / (lam_re * lam_re + lam_im * lam_im)
    zr = ((lb_re - 1.0) * lam_re + lb_im * lam_im) * inv
    zi = (lb_im * lam_re - (lb_re - 1.0) * lam_im) * inv
    bb_re = zr[..., None] * b_re - zi[..., None] * b_im
    bb_im = zr[..., None] * b_im + zi[..., None] * b_re
    magp = jnp.exp(lam_re * dt * power)
    lp_re = magp * jnp.cos(lam_im * dt * power)
    lp_im = magp * jnp.sin(lam_im * dt * power)
    return lb_re, lb_im, lp_re, lp_im, bb_re, bb_im


def _even_layer(x, ln_pre, ln_post, w_in, lam_re, lam_im, b_re, b_im, c_re, c_im, d_skip, log_dt,
                w_glu, b_glu, w_alpha_up, b_alpha, gla_norm, w_out):
    bsz, seq, d = x.shape
    groups, state = lam_re.shape
    gs = b_re.shape[-1]
    sw = groups * gs
    rank, kw = w_alpha_up.shape
    vw = gla_norm.shape[0]
    tile = min(EVEN_TILE, seq)
    lg = S5_LANE_GROUPS
    n_blk = groups // lg
    half = lg * state

    lb_re, lb_im, lp_re, lp_im, bb_re, bb_im = _s5_discretize(
        lam_re.astype(F32), lam_im.astype(F32), b_re.astype(F32), b_im.astype(F32),
        log_dt.astype(F32), float(S5_STEPS))

    def lam_pack(re, im):
        out = jnp.zeros((n_blk, SUBLANES, half), F32)
        return out.at[:, 0].set(re.reshape(n_blk, half)).at[:, 1].set(im.reshape(n_blk, half))

    same_group = (jnp.arange(lg * gs)[:, None] // gs) == (jnp.arange(half)[None, :] // state)

    def b_pack(bb):
        rows = bb.transpose(0, 2, 1).reshape(n_blk, lg * gs, state)
        return jnp.where(same_group, jnp.tile(rows, (1, 1, lg)), 0.0)

    def c_pack(cc):
        rows = cc.transpose(0, 2, 1).reshape(n_blk, half, gs)
        return jnp.where(same_group.T, jnp.tile(rows, (1, 1, lg)), 0.0)

    bmat = jnp.concatenate([b_pack(bb_re), b_pack(bb_im)], axis=2).astype(BF16)
    cmat = jnp.concatenate([c_pack(c_re.astype(F32)), -c_pack(c_im.astype(F32))], axis=1).astype(BF16)
    main = sw + 2 * kw + 2 * vw
    wa = jnp.zeros((d, LANES), F32).at[:, :rank].set(w_in[:, main:main + rank]).astype(BF16)
    wup = jnp.zeros((LANES, kw), F32).at[:rank].set(w_alpha_up).astype(BF16)
    args = (ln_pre.reshape(1, d), ln_post.reshape(1, d), w_in[:, :main].astype(BF16), wa, wup,
            b_alpha.reshape(1, kw), gla_norm.reshape(1, vw),
            lam_pack(lb_re, lb_im), lam_pack(lp_re, lp_im), bmat, cmat,
            d_skip.reshape(1, sw), w_glu.astype(BF16), b_glu.reshape(1, sw), w_out.astype(BF16))
    rows = tile // S5_STEPS
    r_idx = jnp.arange(tile, dtype=jnp.int32)[:, None]
    c_idx = jnp.arange(tile, dtype=jnp.int32)[None, :]
    unperm = (c_idx == (r_idx % S5_STEPS) * rows + r_idx // S5_STEPS)
    sub = min(GLA_SUB, tile)
    r_idx, c_idx = r_idx[:sub], c_idx[:, :sub]
    in_chunk = (r_idx // GLA_CHUNK == c_idx // GLA_CHUNK)
    causal = (in_chunk & (c_idx <= r_idx)).astype(F32)
    args = args + (causal, causal.astype(BF16), in_chunk.astype(BF16), unperm.astype(BF16))
    scratch = [
        pltpu.VMEM((tile, d), BF16),
        pltpu.VMEM((n_blk, tile, LANES), F32),
        pltpu.VMEM((tile, sw), F32),
        pltpu.VMEM((2, tile, 2 * half), F32),
        pltpu.VMEM((2, tile, 2 * half), F32),
        pltpu.VMEM((tile, sw), F32),
        pltpu.VMEM((rows, 2 * half), F32),
        pltpu.VMEM((rows, 2 * half), F32),
        pltpu.VMEM((tile, sw + vw), BF16),
        pltpu.VMEM((n_blk, SUBLANES, half), F32),
        pltpu.VMEM((GLA_HEADS, vw // GLA_HEADS, kw // GLA_HEADS), F32),
    ]
    return pl.pallas_call(
        _even_kernel,
        grid=(bsz, seq // tile),
        in_specs=[_tile_spec(tile, d)] + [_full_spec(a) for a in args],
        out_specs=_tile_spec(tile, d),
        out_shape=jax.ShapeDtypeStruct(x.shape, x.dtype),
        scratch_shapes=scratch,
        compiler_params=_compiler_params(),
        name="even_mixer",
    )(x, *args)


def kernel(x, ln_mix_pre, ln_mix_post, ln_ffn_pre, ln_ffn_post, ev_w_in, s5_lambda_re, s5_lambda_im, s5_b_re, s5_b_im, s5_c_re, s5_c_im, s5_d, s5_log_dt, s5_w_glu, s5_b_glu, gla_w_alpha_up, gla_b_alpha, gla_norm, ev_w_out, od_w_in, ml_conv_w, ml_conv_b, ml_w_q, ml_w_k, ml_w_v, ml_w_gate, ml_b_gate, ml_norm, ml_skip, od_w_out, ffn_w_up, ffn_conv_w, ffn_conv_b, ffn_w_down):
    depth = ln_mix_pre.shape[0]
    for layer in range(depth):
        if layer % 2 == 0:
            e = layer // 2
            x = _even_layer(x, ln_mix_pre[layer], ln_mix_post[layer], ev_w_in[e], s5_lambda_re[e],
                            s5_lambda_im[e], s5_b_re[e], s5_b_im[e], s5_c_re[e], s5_c_im[e], s5_d[e],
                            s5_log_dt[e], s5_w_glu[e], s5_b_glu[e], gla_w_alpha_up[e], gla_b_alpha[e],
                            gla_norm[e], ev_w_out[e])
        else:
            o = layer // 2
            x = _odd_layer(x, ln_mix_pre[layer], ln_mix_post[layer], od_w_in[o], ml_conv_w[o],
                           ml_conv_b[o], ml_w_q[o], ml_w_k[o], ml_w_v[o], ml_w_gate[o], ml_b_gate[o],
                           ml_norm[o], ml_skip[o], od_w_out[o])
        x = _ffn_layer(x, ln_ffn_pre[layer], ln_ffn_post[layer], ffn_w_up[layer], ffn_conv_w[layer],
                       ffn_conv_b[layer], ffn_w_down[layer])
    return x


def _odd_kernel(x_ref, lnpre_ref, lnpost_ref, win_ref, cw_ref, cb_ref, wqk_ref, wv_ref, wfqk_ref, wfv_ref,
                bg_ref,
                mnorm_ref, skip_ref, wout_ref,
                o_ref,
                h_ref, xc_ref, xcb_ref, xmb_ref, q_ref, k_ref, v_ref, tail_ref, c_ref, n_ref,
                m_ref, pkw_ref, pv_ref, pdecay_ref):
    tile = x_ref.shape[0]
    inner = xc_ref.shape[1]
    heads = c_ref.shape[0]
    dh = inner // heads
    bw = wqk_ref.shape[1]
    taps = MLSTM_CONV_TAPS

    @pl.when(pl.program_id(1) == 0)
    def _():
        tail_ref[...] = jnp.zeros_like(tail_ref)
        c_ref[...] = jnp.zeros_like(c_ref)
        n_ref[...] = jnp.zeros_like(n_ref)
        m_ref[...] = jnp.zeros_like(m_ref)
        pkw_ref[...] = jnp.zeros_like(pkw_ref)
        pv_ref[...] = jnp.zeros_like(pv_ref)
        pdecay_ref[...] = jnp.zeros_like(pdecay_ref)

    def apply_pending(hd):
        hs = slice(hd * dh, (hd + 1) * dh)
        c_ref[hd] = (pdecay_ref[hd:hd + 1, 0:1] * c_ref[hd]
                     + _dot_tn(pkw_ref[:, hs], pv_ref[:, hs]))

    for hd in range(heads):
        apply_pending(hd)

    h_ref[...] = _rms(x_ref[...], lnpre_ref[...]).astype(BF16)

    def conv(xm, w):
        out = w[taps - 1:taps] * xm
        for kk in range(taps - 1):
            out = out + w[kk:kk + 1] * pltpu.roll(xm, taps - 1 - kk, axis=0)
        return out

    gw = 2 * bw
    n_groups = inner // gw

    def in_proj(grp):
        return _dot(h_ref[...], win_ref[:, grp * gw:(grp + 1) * gw])

    def conv_silu(grp, xm):
        ls = slice(grp * gw, (grp + 1) * gw)
        w = cw_ref[:, ls]
        b = cb_ref[:, ls]
        pre = conv(xm, w) + b
        head = jnp.concatenate([tail_ref[:, ls], xm[0:SUBLANES]], axis=0)
        pre_head = (conv(head, w) + b)[SUBLANES:2 * SUBLANES]
        tail_ref[:, ls] = xm[tile - SUBLANES:tile]
        xc = jax.nn.silu(jnp.concatenate([pre_head, pre[SUBLANES:]], axis=0))
        xc_ref[:, ls] = xc
        xcb_ref[:, ls] = xc.astype(BF16)
        xmb_ref[:, ls] = xm.astype(BF16)

    def headwise(grp):
        for blk in range(grp * (gw // bw), (grp + 1) * (gw // bw)):
            ls = slice(blk * bw, (blk + 1) * bw)
            qk = _dot(xcb_ref[:, ls], wqk_ref[blk])
            q_ref[:, ls] = qk[:, :bw].astype(BF16)
            k_ref[:, ls] = qk[:, bw:].astype(BF16)
            v_ref[:, ls] = _dot(xmb_ref[:, ls], wv_ref[blk]).astype(BF16)
        gs = slice(grp * gw, (grp + 1) * gw)
        gate_parts.append(_dot(xcb_ref[:, gs], wfqk_ref[gs, :]) + _dot(xmb_ref[:, gs], wfv_ref[gs, :]))

    gate_parts = []

    xm_next = in_proj(0)
    for grp in range(n_groups):
        xm = xm_next
        if grp + 1 < n_groups:
            xm_next = in_proj(grp + 1)
        if grp > 0:
            headwise(grp - 1)
        conv_silu(grp, xm)
    headwise(n_groups - 1)

    cs = min(ODD_CHUNK, tile)
    items = [(ck, hd) for ck in range(tile // cs) for hd in range(heads)]

    def state_matmuls(ck, hd):
        rs = slice(ck * cs, (ck + 1) * cs)
        hs = slice(hd * dh, (hd + 1) * dh)
        qh = q_ref[rs, hs]
        if ck > 0:
            apply_pending(hd)
        return _dot_nt(qh, k_ref[rs, hs]), _dot(qh, c_ref[hd].astype(BF16))

    ahead = state_matmuls(*items[0])

    gates_all = bg_ref[...] + functools.reduce(jnp.add, gate_parts)
    log_f = _log_sigmoid(gates_all)
    row = lax.broadcasted_iota(jnp.int32, (cs, cs), 0)
    col = lax.broadcasted_iota(jnp.int32, (cs, cs), 1)
    causal = col <= row
    tri = jnp.where(causal, 1.0, 0.0).astype(BF16)
    chunk_gates = []
    for ck in range(tile // cs):
        rs = slice(ck * cs, (ck + 1) * cs)
        g_ck = gates_all[rs]
        f_ck = _split_dot(tri, log_f[rs])
        chunk_gates.append((g_ck, f_ck, jnp.transpose(g_ck), jnp.transpose(f_ck)))
    scale = dh ** -0.5

    anchor = None
    for idx, (ck, hd) in enumerate(items):
        qk_raw, inter_raw = ahead
        if idx + 1 < len(items):
            ahead = state_matmuls(*items[idx + 1])
        rs = slice(ck * cs, (ck + 1) * cs)
        hs = slice(hd * dh, (hd + 1) * dh)
        gates, f_cum, gates_t, f_cum_t = chunk_gates[ck]
        f_col = f_cum[:, heads + hd:heads + hd + 1]
        f_row = f_cum_t[heads + hd:heads + hd + 1, :]
        i_col = gates[:, hd:hd + 1]
        i_row = gates_t[hd:hd + 1, :]
        m_prev = m_ref[hd:hd + 1, 0:1]
        if anchor is not None and idx == len(items) - 1:
            m_prev = m_prev + anchor
        d_log = jnp.where(causal, f_col - f_row + i_row, -jnp.inf)
        inter_log = f_col + m_prev
        m_loc = jnp.maximum(inter_log, jnp.max(d_log, axis=-1, keepdims=True))
        qh = q_ref[rs, hs]
        kh = k_ref[rs, hs]
        vh = v_ref[rs, hs]
        s = qk_raw * (scale * jnp.exp(d_log - m_loc))
        w_inter = jnp.exp(inter_log - m_loc)
        num = _dot(s.astype(BF16), vh) + w_inter * inter_raw
        o_pre = _dot(h_ref[rs, :], win_ref[:, inner + hd * dh:inner + (hd + 1) * dh])
        n_vec = n_ref[hd:hd + 1, :]
        den = (jnp.sum(s, axis=-1, keepdims=True)
               + w_inter * jnp.sum(qh.astype(F32) * n_vec, axis=-1, keepdims=True))
        denom = jnp.maximum(jnp.abs(den), jnp.exp(-m_loc))
        hh = num * (1.0 / denom)
        ms = jnp.mean(hh * hh, axis=-1, keepdims=True)
        hn = hh * lax.rsqrt(ms + NORM_EPS) * mnorm_ref[:, hs] + skip_ref[:, hs] * xc_ref[rs, hs]
        gated = (jax.nn.sigmoid(o_pre) * hn).astype(BF16)
        part = _dot(gated, wout_ref[hs, :])
        mix = part if hd == 0 else mix + part
        f_last = f_col[cs - 1:cs, :]
        w_log = f_last - f_col + i_col
        m_new = jnp.maximum(f_last + m_prev, jnp.max(w_log, axis=0, keepdims=True))
        w_upd = jnp.exp(w_log - m_new)
        decay = jnp.exp(f_last + m_prev - m_new)
        kw = kh.astype(F32) * (scale * w_upd)
        pkw_ref[:, hs] = kw.astype(BF16)
        pv_ref[:, hs] = vh
        pdecay_ref[hd:hd + 1, :] = jnp.broadcast_to(decay, (1, pdecay_ref.shape[1]))
        n_ref[hd:hd + 1, :] = decay * n_vec + jnp.sum(kw, axis=0, keepdims=True)
        m_ref[hd:hd + 1, :] = jnp.broadcast_to(m_new, (1, m_ref.shape[1]))
        if hd == heads - 1:
            out = x_ref[rs, :] + _rms(mix, lnpost_ref[...])
            o_ref[rs, :] = out
            if idx + 1 < len(items):
                peak = jnp.max(jnp.max(out, axis=0, keepdims=True), axis=1, keepdims=True)
                anchor = jnp.where(peak != peak, 1.0, 0.0)


def _odd_layer(x, ln_pre, ln_post, w_in, conv_w, conv_b, w_q, w_k, w_v, w_gate, b_gate, m_norm, skip,
               w_out):
    bsz, seq, d = x.shape
    inner = w_out.shape[0]
    heads = b_gate.shape[0] // 2
    tile = min(ODD_TILE, seq)
    bw = MXU_DIM
    n_blocks = inner // bw
    per = bw // MLSTM_QKV_BLOCK
    on_diag = (jnp.arange(bw)[:, None] // MLSTM_QKV_BLOCK) == (jnp.arange(bw)[None, :] // MLSTM_QKV_BLOCK)

    def blockdiag(w):
        rows = w.astype(F32).reshape(n_blocks, bw, MLSTM_QKV_BLOCK)
        return jnp.where(on_diag, jnp.tile(rows, (1, 1, per)), 0.0)

    wqk = jnp.concatenate([blockdiag(w_q), blockdiag(w_k)], axis=2).astype(BF16)
    wv = blockdiag(w_v).astype(BF16)
    nb = inner // MLSTM_QKV_BLOCK
    wg_blocks = w_gate.astype(F32).reshape(3, nb, MLSTM_QKV_BLOCK, 2 * heads)

    def fold(w, g):
        return jnp.einsum('ncd,ndg->ncg', w.astype(F32), g,
                          precision=lax.Precision.HIGHEST).reshape(inner, 2 * heads)

    def pad_lanes(w):
        return jnp.zeros((inner, LANES), F32).at[:, :2 * heads].set(w).astype(BF16)

    wfqk = pad_lanes(fold(w_q, wg_blocks[0]) + fold(w_k, wg_blocks[1]))
    wfv = pad_lanes(fold(w_v, wg_blocks[2]))
    bg = jnp.zeros((1, LANES), F32).at[0, :2 * heads].set(b_gate)
    cw = jnp.zeros((SUBLANES, inner), F32).at[:conv_w.shape[0]].set(conv_w)
    args = (ln_pre.reshape(1, d), ln_post.reshape(1, d), w_in.astype(BF16), cw, conv_b.reshape(1, inner),
            wqk, wv, wfqk, wfv, bg, m_norm.reshape(1, inner), skip.reshape(1, inner), w_out.astype(BF16))
    dh = inner // heads
    scratch = [
        pltpu.VMEM((tile, d), BF16),
        pltpu.VMEM((tile, inner), F32),
        pltpu.VMEM((tile, inner), BF16),
        pltpu.VMEM((tile, inner), BF16),
        pltpu.VMEM((tile, inner), BF16),
        pltpu.VMEM((tile, inner), BF16),
        pltpu.VMEM((tile, inner), BF16),
        pltpu.VMEM((SUBLANES, inner), F32),
        pltpu.VMEM((heads, dh, dh), F32),
        pltpu.VMEM((SUBLANES, dh), F32),
        pltpu.VMEM((SUBLANES, LANES), F32),
        pltpu.VMEM((min(ODD_CHUNK, tile), inner), BF16),
        pltpu.VMEM((min(ODD_CHUNK, tile), inner), BF16),
        pltpu.VMEM((SUBLANES, LANES), F32),
    ]
    return pl.pallas_call(
        _odd_kernel,
        grid=(bsz, seq // tile),
        in_specs=[_tile_spec(tile, d)] + [_full_spec(a) for a in args],
        out_specs=_tile_spec(tile, d),
        out_shape=jax.ShapeDtypeStruct(x.shape, x.dtype),
        scratch_shapes=scratch,
        compiler_params=_compiler_params(),
        name="odd_mixer",
    )(x, *args)
```
